```python
import jax, jax.numpy as jnp
from jax import lax
import numpy as np

D_MODEL = 1024
BATCH = 4
SEQ = 4096
DEPTH = 2

N_DENSE = (DEPTH + 1) // 2
N_MOE = DEPTH // 2

MLA_HEADS = 8
MLA_NOPE = 64
MLA_ROPE = 32
MLA_QK = MLA_NOPE + MLA_ROPE
MLA_V = 64
MLA_Q_RANK = D_MODEL // 4
MLA_KV_RANK = D_MODEL // 4
ROPE_BASE = 10000.0
Q_BLOCK = 128
GLA_HEADS = 4
GLA_DK = 64
GLA_DV = 128
GLA_GATE_RANK = 16
GLA_TAU = 16.0
GLA_CHUNK = 16
POOL_WINDOWS = (2, 4, 8, 16)
POOL_GROUP = 128
POOL_WIDTH = 4 * POOL_GROUP
N_BRANCH = 3
BRANCH_WIDTH = 512
D_FF = 2816
N_EXPERTS = 8
TOP_K = 2
D_EXPERT = 1408
EPS = 1e-6

IN_SPLITS = (MLA_Q_RANK, MLA_KV_RANK, MLA_ROPE,
             GLA_HEADS * GLA_DK, GLA_HEADS * GLA_DK, GLA_HEADS * GLA_DV, GLA_GATE_RANK, GLA_HEADS * GLA_DV,
             POOL_WIDTH, N_BRANCH * D_MODEL)
D_IN = sum(IN_SPLITS)

kernel_name = 'hybrid_mla_gla_pool_gated_moe'


def rms_norm(x, g):
    xf = x.astype(jnp.float32)
    y = xf * lax.rsqrt(jnp.mean(xf * xf, axis=-1, keepdims=True) + EPS)
    return (y * g.astype(jnp.float32)).astype(x.dtype)


def rope(x, positions):
    half = x.shape[-1] // 2
    inv_freq = ROPE_BASE ** (-jnp.arange(half, dtype=jnp.float32) / half)
    ang = positions.astype(jnp.float32)[..., None] * inv_freq
    cos = jnp.cos(ang)[:, :, None, :]
    sin = jnp.sin(ang)[:, :, None, :]
    xf = x.astype(jnp.float32)
    x1, x2 = xf[..., :half], xf[..., half:]
    return jnp.concatenate([x1 * cos - x2 * sin, x2 * cos + x1 * sin], axis=-1).astype(x.dtype)


def causal_block_attention(q, k, v):
    B, S, H, Dq = q.shape
    nb = S // Q_BLOCK
    scale = Dq ** -0.5
    key_pos = jnp.arange(S)
    q_blocks = q.reshape(B, nb, Q_BLOCK, H, Dq).transpose(1, 0, 2, 3, 4)

    def one_block(args):
        i, qb = args
        s = jnp.einsum('bqhd,bkhd->bhqk', qb, k, preferred_element_type=jnp.float32) * scale
        q_pos = i * Q_BLOCK + jnp.arange(Q_BLOCK)
        s = jnp.where(key_pos[None, :] <= q_pos[:, None], s, -jnp.inf)
        p = jax.nn.softmax(s, axis=-1).astype(v.dtype)
        return jnp.einsum('bhqk,bkhv->bqhv', p, v)

    o = lax.map(one_block, (jnp.arange(nb), q_blocks))
    return o.transpose(1, 0, 2, 3, 4).reshape(B, S, H, v.shape[-1])


def mla_branch(c_q, c_kv, k_rope, positions, g_cq, w_uq, g_ckv, w_ukv, g_qk_q, g_qk_k):
    B, S, _ = c_q.shape
    q = (rms_norm(c_q, g_cq) @ w_uq).reshape(B, S, MLA_HEADS, MLA_QK)
    kv = (rms_norm(c_kv, g_ckv) @ w_ukv).reshape(B, S, MLA_HEADS, MLA_NOPE + MLA_V)
    k_nope, v = kv[..., :MLA_NOPE], kv[..., MLA_NOPE:]
    k_r = jnp.broadcast_to(k_rope[:, :, None, :], (B, S, MLA_HEADS, MLA_ROPE))
    k = jnp.concatenate([k_nope, k_r], axis=-1)
    q = rms_norm(q, g_qk_q)
    k = rms_norm(k, g_qk_k)
    q = jnp.concatenate([q[..., :MLA_NOPE], rope(q[..., MLA_NOPE:], positions)], axis=-1)
    k = jnp.concatenate([k[..., :MLA_NOPE], rope(k[..., MLA_NOPE:], positions)], axis=-1)
    o = causal_block_attention(q, k, v)
    return o.reshape(B, S, MLA_HEADS * MLA_V)


def gla_branch(q, k, v, gate_lr, out_gate, w_gate_up, b_gate, g_out):
    B, S, _ = q.shape
    N, C, H = S // GLA_CHUNK, GLA_CHUNK, GLA_HEADS
    f32 = jnp.float32
    log_a = jax.nn.log_sigmoid((gate_lr @ w_gate_up + b_gate).astype(f32)) / GLA_TAU
    qc = q.astype(f32).reshape(B, N, C, H, GLA_DK) * (GLA_DK ** -0.5)
    kc = k.astype(f32).reshape(B, N, C, H, GLA_DK)
    vc = v.astype(f32).reshape(B, N, C, H, GLA_DV)
    bc = jnp.cumsum(log_a.reshape(B, N, C, H, GLA_DK), axis=2)
    causal = jnp.tril(jnp.ones((C, C), dtype=bool))[None, None, :, :, None, None]
    diff = bc[:, :, :, None] - bc[:, :, None, :]
    decay = jnp.exp(jnp.where(causal, diff, -jnp.inf))
    attn = jnp.einsum('bnijhd,bnjhd->bnhij', qc[:, :, :, None] * decay, kc)
    o_intra = jnp.einsum('bnhij,bnjhv->bnihv', attn, vc)
    b_last = bc[:, :, -1]
    q_dec = qc * jnp.exp(bc)
    k_dec = kc * jnp.exp(b_last[:, :, None] - bc)
    a_last = jnp.exp(b_last)

    def step(state, xs):
        qd, kd, vv, al = xs
        o = jnp.einsum('bihd,bhdv->bihv', qd, state)
        state = al[..., None] * state + jnp.einsum('bjhd,bjhv->bhdv', kd, vv)
        return state, o

    xs = (jnp.moveaxis(q_dec, 1, 0), jnp.moveaxis(k_dec, 1, 0), jnp.moveaxis(vc, 1, 0), jnp.moveaxis(a_last, 1, 0))
    state0 = jnp.zeros((B, H, GLA_DK, GLA_DV), f32)
    _, o_inter = lax.scan(step, state0, xs)
    o = o_intra + jnp.moveaxis(o_inter, 0, 1)
    o = rms_norm(o.reshape(B, S, H, GLA_DV), g_out).astype(q.dtype)
    o = o * jax.nn.silu(out_gate.reshape(B, S, H, GLA_DV))
    return o.reshape(B, S, H * GLA_DV)


def pool_branch(u, w_pool, pool_scale):
    B, S, _ = u.shape
    uf = u.astype(jnp.float32).reshape(B, S, len(POOL_WINDOWS), POOL_GROUP)
    cs = jnp.cumsum(uf, axis=1)
    t1 = jnp.arange(1, S + 1, dtype=jnp.float32)
    outs = []
    for gi, w in enumerate(POOL_WINDOWS):
        c = cs[:, :, gi]
        c_shift = jnp.pad(c, ((0, 0), (w, 0), (0, 0)))[:, :S]
        mean = (c - c_shift) / jnp.minimum(t1, w)[None, :, None]
        outs.append(mean - uf[:, :, gi])
    pooled = jnp.stack(outs, axis=2).astype(u.dtype)
    y = jnp.einsum('bsgc,gcd->bsgd', pooled, w_pool).reshape(B, S, POOL_WIDTH)
    return y * pool_scale


def swiglu(h, w_gate, w_up, w_down):
    return (jax.nn.silu(h @ w_gate) * (h @ w_up)) @ w_down


def moe_ffn(h, w_router, w_e_gate, w_e_up, w_e_down):
    logits = jnp.einsum('bsd,de->bse', h, w_router, preferred_element_type=jnp.float32)
    top_vals, top_idx = lax.top_k(logits, TOP_K)
    top_w = jax.nn.softmax(top_vals, axis=-1)
    combine = jnp.sum(jax.nn.one_hot(top_idx, N_EXPERTS, dtype=jnp.float32) * top_w[..., None], axis=-2)
    combine = combine.astype(h.dtype)
    y = jnp.zeros_like(h)
    for e in range(N_EXPERTS):
        y = y + combine[..., e:e + 1] * swiglu(h, w_e_gate[e], w_e_up[e], w_e_down[e])
    return y


def setup_inputs(seed: int = 0) -> dict:
    key = jax.random.key(seed)
    ks = jax.random.split(key, 32)
    f32 = jnp.float32

    def w(k, shape, fan_in):
        return jax.random.normal(k, shape, f32) * fan_in ** -0.5

    def gain(k, shape):
        return 1.0 + 0.05 * jax.random.normal(k, shape, f32)

    x = jax.random.normal(ks[0], (BATCH, SEQ, D_MODEL), f32)
    offset = jax.random.randint(ks[1], (BATCH, 1), 0, 1024, dtype=jnp.int32)
    positions = jnp.arange(SEQ, dtype=jnp.int32)[None, :] + offset
    return {
        'x': x,
        'positions': positions,
        'g_mix': gain(ks[2], (DEPTH, D_MODEL)),
        'w_in': w(ks[3], (DEPTH, D_MODEL, D_IN), D_MODEL),
        'g_cq': gain(ks[4], (DEPTH, MLA_Q_RANK)),
        'w_uq': w(ks[5], (DEPTH, MLA_Q_RANK, MLA_HEADS * MLA_QK), MLA_Q_RANK),
        'g_ckv': gain(ks[6], (DEPTH, MLA_KV_RANK)),
        'w_ukv': w(ks[7], (DEPTH, MLA_KV_RANK, MLA_HEADS * (MLA_NOPE + MLA_V)), MLA_KV_RANK),
        'g_qk_q': gain(ks[8], (DEPTH, MLA_QK)),
        'g_qk_k': gain(ks[9], (DEPTH, MLA_QK)),
        'w_gla_gate_up': w(ks[10], (DEPTH, GLA_GATE_RANK, GLA_HEADS * GLA_DK), GLA_GATE_RANK),
        'b_gla_gate': 0.1 * jax.random.normal(ks[11], (DEPTH, GLA_HEADS * GLA_DK), f32),
        'g_gla_out': gain(ks[12], (DEPTH, GLA_DV)),
        'w_pool': w(ks[13], (DEPTH, len(POOL_WINDOWS), POOL_GROUP, POOL_GROUP), POOL_GROUP),
        'pool_scale': gain(ks[14], (DEPTH, POOL_WIDTH)),
        'w_branch': w(ks[15], (DEPTH, N_BRANCH, BRANCH_WIDTH, D_MODEL), BRANCH_WIDTH),
        'w_out': w(ks[16], (DEPTH, D_MODEL, D_MODEL), D_MODEL),
        'g_ffn': gain(ks[17], (DEPTH, D_MODEL)),
        'w_ffn_gate': w(ks[18], (N_DENSE, D_MODEL, D_FF), D_MODEL),
        'w_ffn_up': w(ks[19], (N_DENSE, D_MODEL, D_FF), D_MODEL),
        'w_ffn_down': w(ks[20], (N_DENSE, D_FF, D_MODEL), D_FF),
        'w_router': w(ks[21], (N_MOE, D_MODEL, N_EXPERTS), D_MODEL),
        'w_exp_gate': w(ks[22], (N_MOE, N_EXPERTS, D_MODEL, D_EXPERT), D_MODEL),
        'w_exp_up': w(ks[23], (N_MOE, N_EXPERTS, D_MODEL, D_EXPERT), D_MODEL),
        'w_exp_down': w(ks[24], (N_MOE, N_EXPERTS, D_EXPERT, D_MODEL), D_EXPERT),
    }


def reference(x, positions, g_mix, w_in, g_cq, w_uq, g_ckv, w_ukv, g_qk_q, g_qk_k, w_gla_gate_up, b_gla_gate,
              g_gla_out, w_pool, pool_scale, w_branch, w_out, g_ffn, w_ffn_gate, w_ffn_up, w_ffn_down,
              w_router, w_exp_gate, w_exp_up, w_exp_down):
    B, S, _ = x.shape
    split_points = np.cumsum(IN_SPLITS)[:-1].tolist()
    for layer in range(DEPTH):
        h = rms_norm(x, g_mix[layer])
        proj = h @ w_in[layer]
        (c_q, c_kv, k_rope, gla_q, gla_k, gla_v, gla_lr, gla_og, u_pool, gate_logits) = jnp.split(proj, split_points, axis=-1)
        y_a = mla_branch(c_q, c_kv, k_rope, positions, g_cq[layer], w_uq[layer], g_ckv[layer], w_ukv[layer],
                         g_qk_q[layer], g_qk_k[layer])
        y_b = gla_branch(gla_q, gla_k, gla_v, gla_lr, gla_og, w_gla_gate_up[layer], b_gla_gate[layer], g_gla_out[layer])
        y_c = pool_branch(u_pool, w_pool[layer], pool_scale[layer])
        ys = jnp.stack([y_a, y_b, y_c], axis=2)
        y_d = jnp.einsum('bsnc,ncd->bsnd', ys, w_branch[layer])
        gates = jax.nn.sigmoid(gate_logits.astype(jnp.float32).reshape(B, S, N_BRANCH, D_MODEL)).astype(x.dtype)
        merged = jnp.sum(gates * y_d, axis=2)
        x = x + merged @ w_out[layer]
        h = rms_norm(x, g_ffn[layer])
        i = layer // 2
        if layer % 2 == 0:
            x = x + swiglu(h, w_ffn_gate[i], w_ffn_up[i], w_ffn_down[i])
        else:
            x = x + moe_ffn(h, w_router[i], w_exp_gate[i], w_exp_up[i], w_exp_down[i])
    return x
```

```python
import functools

import jax
import jax.numpy as jnp
import numpy as np
from jax import lax
from jax.experimental import pallas as pl
from jax.experimental.pallas import tpu as pltpu

F32 = jnp.float32
BF16 = jnp.bfloat16
HIGHEST = lax.Precision.HIGHEST

D_MODEL = 1024
MLA_HEADS = 8
MLA_NOPE = 64
MLA_ROPE = 32
MLA_QK = MLA_NOPE + MLA_ROPE
MLA_V = 64
MLA_Q_RANK = D_MODEL // 4
MLA_KV_RANK = D_MODEL // 4
ROPE_BASE = 10000.0
GLA_HEADS = 4
GLA_DK = 64
GLA_DV = 128
GLA_GATE_RANK = 16
GLA_TAU = 16.0
POOL_WINDOWS = (2, 4, 8, 16)
POOL_GROUP = 128
POOL_WIDTH = 4 * POOL_GROUP
N_BRANCH = 3
BRANCH_WIDTH = 512
D_FF = 2816
N_EXPERTS = 8
TOP_K = 2
D_EXPERT = 1408
EPS = 1e-6

IN_SPLITS = (MLA_Q_RANK, MLA_KV_RANK, MLA_ROPE,
             GLA_HEADS * GLA_DK, GLA_HEADS * GLA_DK, GLA_HEADS * GLA_DV, GLA_GATE_RANK, GLA_HEADS * GLA_DV,
             POOL_WIDTH, N_BRANCH * D_MODEL)

LANES = 128
HEAD_PAD = LANES

COL_GATES = 0
COL_CQ = 3072
COL_CKV = 3328
COL_GQ = 3584
COL_GK = 3840
COL_GV = 4096
COL_GOG = 4608
COL_POOL = 5120
COL_SMALL = 5632
D_PROJ = 5760
SMALL_LR = MLA_ROPE

TM_PROJ = 1024
TN_PROJ = 640
ATT_TQ = 512
ATT_TK = 512
GLA_L = 128
GLA_BASE = 1
TM_POOL = 256
POOL_HALO = 16
TM_MERGE = 512
TM_FFN = 1024
TF_FFN = 256
TM_MOE = 512
VMEM_LIMIT = 56 * 1024 * 1024


def _cparams(*sem):
    return pltpu.CompilerParams(dimension_semantics=sem, vmem_limit_bytes=VMEM_LIMIT)


def _tile(n, pref):
    t = min(n, pref)
    assert n % t == 0, (n, pref)
    return t


def _rms(x):
    return x * lax.rsqrt(jnp.mean(x * x, axis=-1, keepdims=True) + EPS)


_NT = (((1,), (1,)), ((), ()))
_TN = (((0,), (0,)), ((), ()))


def _norm_matmul_body(x_ref, g_ref, w_ref, o_ref, h_ref):
    @pl.when(pl.program_id(1) == 0)
    def _():
        h_ref[...] = (_rms(x_ref[...]) * g_ref[...]).astype(BF16)

    o_ref[...] = jnp.dot(h_ref[...], w_ref[...], preferred_element_type=F32).astype(o_ref.dtype)


def _norm_matmul(x2d, g, w, out_dtype):
    t, d = x2d.shape
    n = w.shape[1]
    tm, tn = _tile(t, TM_PROJ), _tile(n, TN_PROJ)
    return pl.pallas_call(
        _norm_matmul_body,
        grid=(t // tm, n // tn),
        in_specs=[pl.BlockSpec((tm, d), lambda i, j: (i, 0)),
                  pl.BlockSpec((1, d), lambda i, j: (0, 0)),
                  pl.BlockSpec((d, tn), lambda i, j: (0, j))],
        out_specs=pl.BlockSpec((tm, tn), lambda i, j: (i, j)),
        out_shape=jax.ShapeDtypeStruct((t, n), out_dtype),
        scratch_shapes=[pltpu.VMEM((tm, d), BF16)],
        compiler_params=_cparams("parallel", "arbitrary"),
        name="norm_in_proj",
    )(x2d, g.reshape(1, d), w)


def _mla_prep_body(pos_ref, cq_ref, ckv_ref, small_ref, gcq_ref, wuq_ref, gckv_ref, wukv_ref, gq_ref, gk_ref,
                   invf_ref, qT_ref, k_ref, vT_ref):
    tp = cq_ref.shape[1]
    half = MLA_ROPE // 2
    ang = invf_ref[...] * pos_ref[0].astype(F32)
    cos, sin = jnp.cos(ang), jnp.sin(ang)

    def rope(t):
        x1, x2 = t[:half], t[half:]
        return x1 * cos - x2 * sin, x2 * cos + x1 * sin

    cqn = (_rms(cq_ref[0].astype(F32)) * gcq_ref[...]).astype(BF16)
    ckvn = (_rms(ckv_ref[0].astype(F32)) * gckv_ref[...]).astype(BF16)
    qT = lax.dot_general(wuq_ref[...], cqn, _NT, preferred_element_type=F32)
    kvT = lax.dot_general(wukv_ref[...], ckvn, _NT, preferred_element_type=F32)
    kr = small_ref[0].astype(F32).T[:MLA_ROPE]
    kr_ss = jnp.sum(kr * kr, axis=0, keepdims=True)
    gq, gk = gq_ref[...], gk_ref[...]
    zpad = jnp.zeros((HEAD_PAD - MLA_QK, tp), F32)
    scale = MLA_QK ** -0.5
    for h in range(MLA_HEADS):
        qh = qT[h * MLA_QK:(h + 1) * MLA_QK]
        qn = qh * lax.rsqrt(jnp.mean(qh * qh, axis=0, keepdims=True) + EPS) * (gq * scale)
        r1, r2 = rope(qn[MLA_NOPE:])
        qT_ref[0, h] = jnp.concatenate([qn[:MLA_NOPE], r1, r2, zpad], axis=0).astype(BF16)
        base = h * (MLA_NOPE + MLA_V)
        kn = kvT[base:base + MLA_NOPE]
        r = lax.rsqrt((jnp.sum(kn * kn, axis=0, keepdims=True) + kr_ss) * (1.0 / MLA_QK) + EPS)
        r1, r2 = rope(kr * r * gk[MLA_NOPE:])
        kfm = jnp.concatenate([kn * r * gk[:MLA_NOPE], r1, r2, zpad], axis=0)
        k_ref[0, h] = kfm.T.astype(BF16)
        vT_ref[0, h, 0] = kvT[base + MLA_NOPE:base + MLA_NOPE + MLA_V].astype(BF16)


def _mla_prep(proj, positions, g_cq, w_uq, g_ckv, w_ukv, g_qk_q, g_qk_k):
    b, s, _ = proj.shape
    tp = _tile(s, ATT_TK)
    ns = s // tp
    h = MLA_HEADS
    pos = positions.reshape(b * ns, 1, tp)
    inv_freq = (ROPE_BASE ** (-jnp.arange(MLA_ROPE // 2, dtype=F32) / (MLA_ROPE // 2))).reshape(-1, 1)
    const = lambda shape: pl.BlockSpec(shape, lambda bi, si: (0,) * len(shape))
    return pl.pallas_call(
        _mla_prep_body,
        grid=(b, ns),
        in_specs=[pl.BlockSpec((1, 1, tp), lambda bi, si: (bi * ns + si, 0, 0)),
                  pl.BlockSpec((1, tp, MLA_Q_RANK), lambda bi, si: (bi, si, COL_CQ // MLA_Q_RANK)),
                  pl.BlockSpec((1, tp, MLA_KV_RANK), lambda bi, si: (bi, si, COL_CKV // MLA_KV_RANK)),
                  pl.BlockSpec((1, tp, LANES), lambda bi, si: (bi, si, COL_SMALL // LANES)),
                  const((1, MLA_Q_RANK)), const((h * MLA_QK, MLA_Q_RANK)),
                  const((1, MLA_KV_RANK)), const((h * (MLA_NOPE + MLA_V), MLA_KV_RANK)),
                  const((MLA_QK, 1)), const((MLA_QK, 1)), const((MLA_ROPE // 2, 1))],
        out_specs=[pl.BlockSpec((1, h, HEAD_PAD, tp), lambda bi, si: (bi, 0, 0, si)),
                   pl.BlockSpec((1, h, tp, HEAD_PAD), lambda bi, si: (bi, 0, si, 0)),
                   pl.BlockSpec((1, h, 1, MLA_V, tp), lambda bi, si: (bi, 0, si, 0, 0))],
        out_shape=[jax.ShapeDtypeStruct((b, h, HEAD_PAD, s), BF16),
                   jax.ShapeDtypeStruct((b, h, s, HEAD_PAD), BF16),
                   jax.ShapeDtypeStruct((b, h, ns, MLA_V, tp), BF16)],
        compiler_params=_cparams("parallel", "parallel"),
        name="mla_prep",
    )(pos, proj, proj, proj, g_cq.reshape(1, -1), w_uq.T.astype(BF16), g_ckv.reshape(1, -1), w_ukv.T.astype(BF16),
      g_qk_q.reshape(-1, 1), g_qk_k.reshape(-1, 1), inv_freq)


def _attn_body(qT_ref, k_ref, vT_ref, o_ref, *, tq, tk):
    i = pl.program_id(2)
    qT = qT_ref[0, 0]

    def step(j, carry, masked):
        m, l, acc = carry
        kb = k_ref[0, 0, pl.ds(pl.multiple_of(j * tk, tk), tk), :]
        s = jnp.dot(kb, qT, preferred_element_type=F32)
        if masked:
            kpos = j * tk + lax.broadcasted_iota(jnp.int32, (tk, tq), 0)
            qpos = i * tq + lax.broadcasted_iota(jnp.int32, (tk, tq), 1)
            s = jnp.where(kpos <= qpos, s, -jnp.inf)
        m_new = jnp.maximum(m, jnp.max(s, axis=0, keepdims=True))
        alpha = jnp.exp(m - m_new)
        p = jnp.exp(s - m_new)
        l = alpha * l + jnp.sum(p, axis=0, keepdims=True)
        acc = alpha * acc + jnp.dot(vT_ref[0, 0, j], p.astype(BF16), preferred_element_type=F32)
        return m_new, l, acc

    carry = (jnp.full((1, tq), -jnp.inf, F32), jnp.zeros((1, tq), F32), jnp.zeros((MLA_V, tq), F32))
    ratio = tq // tk
    carry = lax.fori_loop(0, i * ratio, functools.partial(step, masked=False), carry)
    for jj in range(ratio):
        carry = step(i * ratio + jj, carry, True)
    _, l, acc = carry
    o_ref[0, 0] = (acc / l).astype(o_ref.dtype)


def _attention(qT, k, vT):
    b, h, _, s = qT.shape
    tk = vT.shape[-1]
    tq = _tile(s, ATT_TQ)
    nk = s // tk
    return pl.pallas_call(
        functools.partial(_attn_body, tq=tq, tk=tk),
        grid=(b, h, s // tq),
        in_specs=[pl.BlockSpec((1, 1, HEAD_PAD, tq), lambda bi, hi, qi: (bi, hi, 0, qi)),
                  pl.BlockSpec((1, 1, s, HEAD_PAD), lambda bi, hi, qi: (bi, hi, 0, 0)),
                  pl.BlockSpec((1, 1, nk, MLA_V, tk), lambda bi, hi, qi: (bi, hi, 0, 0, 0))],
        out_specs=pl.BlockSpec((1, 1, MLA_V, tq), lambda bi, hi, qi: (bi, hi, 0, qi)),
        out_shape=jax.ShapeDtypeStruct((b, h, MLA_V, s), BF16),
        compiler_params=_cparams("parallel", "parallel", "arbitrary"),
        name="mla_attention",
    )(qT, k, vT)


def _gla_body(q_ref, k_ref, v_ref, og_ref, small_ref, wg_ref, bg_ref, gout_ref, o_ref, st_ref, *, blk):
    @pl.when(pl.program_id(1) == 0)
    def _():
        st_ref[...] = jnp.zeros_like(st_ref)

    hd = GLA_HEADS * GLA_DK
    lr = small_ref[0][:, SMALL_LR:SMALL_LR + GLA_GATE_RANK].astype(F32)
    z = jnp.dot(lr, wg_ref[...], precision=HIGHEST, preferred_element_type=F32) + bg_ref[...]
    log_a = jax.nn.log_sigmoid(z) * (1.0 / GLA_TAU)
    row = lax.broadcasted_iota(jnp.int32, (blk, blk), 0)
    col = lax.broadcasted_iota(jnp.int32, (blk, blk), 1)
    bc = jnp.dot((col <= row).astype(F32), log_a, precision=HIGHEST, preferred_element_type=F32)
    q = q_ref[0].astype(F32) * (GLA_DK ** -0.5)
    k = k_ref[0].astype(F32)

    def ref_rows(group, off):
        parts = [jnp.broadcast_to(bc[g0 + off:g0 + off + 1], (group, hd)) for g0 in range(0, blk, group)]
        return parts[0] if len(parts) == 1 else jnp.concatenate(parts, axis=0)

    levels = []
    group = blk
    while group > GLA_BASE:
        half = group // 2
        ref = ref_rows(group, half - 1)
        qs = (q * jnp.exp(jnp.minimum(bc - ref, 0.0))).astype(BF16)
        ks = (k * jnp.exp(jnp.minimum(ref - bc, 0.0))).astype(BF16)
        mask = ((row & -group) == (col & -group)) & ((row & half) != 0) & ((col & half) == 0)
        levels.append((qs, ks, mask))
        group = half
    if GLA_BASE == 1:
        levels.append((q.astype(BF16), k.astype(BF16), row == col))
    else:
        ref = ref_rows(GLA_BASE, GLA_BASE // 2 - 1)
        mask = ((row & -GLA_BASE) == (col & -GLA_BASE)) & (col <= row)
        levels.append(((q * jnp.exp(bc - ref)).astype(BF16), (k * jnp.exp(ref - bc)).astype(BF16), mask))

    qd = (q * jnp.exp(bc)).astype(BF16)
    b_last = bc[blk - 1:blk]
    kd = (k * jnp.exp(b_last - bc)).astype(BF16)
    a_last = jnp.exp(b_last)
    gout = gout_ref[...]
    for h in range(GLA_HEADS):
        ks_, vs_ = slice(h * GLA_DK, (h + 1) * GLA_DK), slice(h * GLA_DV, (h + 1) * GLA_DV)
        attn = jnp.zeros((blk, blk), F32)
        for qs, ksc, mask in levels:
            p = lax.dot_general(qs[:, ks_], ksc[:, ks_], _NT, preferred_element_type=F32)
            attn = jnp.where(mask, p, attn)
        v_h = v_ref[0, :, vs_]
        st = st_ref[h]
        o = jnp.dot(attn.astype(BF16), v_h, preferred_element_type=F32)
        o = o + lax.dot_general(qd[:, ks_], st.astype(BF16), _NT, preferred_element_type=F32)
        st_ref[h] = st * a_last[:, ks_] + lax.dot_general(v_h, kd[:, ks_], _TN, preferred_element_type=F32)
        og = og_ref[0, :, vs_].astype(F32)
        o_ref[0, :, vs_] = (_rms(o) * gout * (og * jax.nn.sigmoid(og))).astype(o_ref.dtype)


def _gla(proj, w_gate_up, b_gate, g_out):
    b, s, _ = proj.shape
    blk = _tile(s, GLA_L)
    hd, hv = GLA_HEADS * GLA_DK, GLA_HEADS * GLA_DV
    const = lambda shape: pl.BlockSpec(shape, lambda bi, si: (0,) * len(shape))
    return pl.pallas_call(
        functools.partial(_gla_body, blk=blk),
        grid=(b, s // blk),
        in_specs=[pl.BlockSpec((1, blk, hd), lambda bi, si: (bi, si, COL_GQ // hd)),
                  pl.BlockSpec((1, blk, hd), lambda bi, si: (bi, si, COL_GK // hd)),
                  pl.BlockSpec((1, blk, hv), lambda bi, si: (bi, si, COL_GV // hv)),
                  pl.BlockSpec((1, blk, hv), lambda bi, si: (bi, si, COL_GOG // hv)),
                  pl.BlockSpec((1, blk, LANES), lambda bi, si: (bi, si, COL_SMALL // LANES)),
                  const((GLA_GATE_RANK, hd)), const((1, hd)), const((1, GLA_DV))],
        out_specs=pl.BlockSpec((1, blk, hv), lambda bi, si: (bi, si, 0)),
        out_shape=jax.ShapeDtypeStruct((b, s, hv), BF16),
        scratch_shapes=[pltpu.VMEM((GLA_HEADS, GLA_DV, GLA_DK), F32)],
        compiler_params=_cparams("parallel", "arbitrary"),
        name="gla",
    )(proj, proj, proj, proj, proj, w_gate_up, b_gate.reshape(1, -1), g_out.reshape(1, -1))


def _pool_body(u_ref, halo_ref, w_ref, scale_ref, o_ref, *, tm):
    si = pl.program_id(1)
    row = lax.broadcasted_iota(jnp.int32, (tm, tm + POOL_HALO), 0)
    col = lax.broadcasted_iota(jnp.int32, (tm, tm + POOL_HALO), 1)
    t1 = (si * tm + lax.broadcasted_iota(jnp.int32, (tm, 1), 0) + 1).astype(F32)
    halo_on = (si > 0).astype(F32)
    for g, win in enumerate(POOL_WINDOWS):
        cs = slice(g * POOL_GROUP, (g + 1) * POOL_GROUP)
        cur = u_ref[0, :, cs].astype(F32)
        ext = jnp.concatenate([halo_ref[0, :, cs].astype(F32) * halo_on, cur], axis=0)
        band = ((col <= row + POOL_HALO) & (col > row + POOL_HALO - win)).astype(F32)
        wsum = jnp.dot(band, ext, precision=HIGHEST, preferred_element_type=F32)
        pooled = wsum / jnp.minimum(t1, float(win)) - cur
        y = jnp.dot(pooled.astype(BF16), w_ref[g], preferred_element_type=F32)
        o_ref[0, :, cs] = (y * scale_ref[:, cs]).astype(o_ref.dtype)


def _pool(proj, w_pool, pool_scale):
    b, s, _ = proj.shape
    tm = _tile(s, TM_POOL)
    hb = tm // POOL_HALO
    return pl.pallas_call(
        functools.partial(_pool_body, tm=tm),
        grid=(b, s // tm),
        in_specs=[pl.BlockSpec((1, tm, POOL_WIDTH), lambda bi, si: (bi, si, COL_POOL // POOL_WIDTH)),
                  pl.BlockSpec((1, POOL_HALO, POOL_WIDTH),
                               lambda bi, si: (bi, jnp.maximum(si * hb - 1, 0), COL_POOL // POOL_WIDTH)),
                  pl.BlockSpec((len(POOL_WINDOWS), POOL_GROUP, POOL_GROUP), lambda bi, si: (0, 0, 0)),
                  pl.BlockSpec((1, POOL_WIDTH), lambda bi, si: (0, 0))],
        out_specs=pl.BlockSpec((1, tm, POOL_WIDTH), lambda bi, si: (bi, si, 0)),
        out_shape=jax.ShapeDtypeStruct((b, s, POOL_WIDTH), BF16),
        compiler_params=_cparams("parallel", "parallel"),
        name="pool",
    )(proj, proj, w_pool.astype(BF16), pool_scale.reshape(1, -1))


def _merge_body(x_ref, yaT_ref, yb_ref, yc_ref, g0_ref, g1_ref, g2_ref, wb_ref, wo_ref, o_ref):
    d_a = lax.dot_general(yaT_ref[0], wb_ref[0], _TN, preferred_element_type=F32)
    d_b = jnp.dot(yb_ref[0], wb_ref[1], preferred_element_type=F32)
    d_c = jnp.dot(yc_ref[0], wb_ref[2], preferred_element_type=F32)
    merged = (jax.nn.sigmoid(g0_ref[0].astype(F32)) * d_a + jax.nn.sigmoid(g1_ref[0].astype(F32)) * d_b
              + jax.nn.sigmoid(g2_ref[0].astype(F32)) * d_c)
    o_ref[0] = x_ref[0] + jnp.dot(merged.astype(BF16), wo_ref[...], preferred_element_type=F32)


def _merge(x, yaT, yb, yc, proj, w_branch, w_out):
    b, s, d = x.shape
    tm = _tile(s, TM_MERGE)
    tok = lambda width, cb: pl.BlockSpec((1, tm, width), lambda bi, si: (bi, si, cb))
    return pl.pallas_call(
        _merge_body,
        grid=(b, s // tm),
        in_specs=[tok(d, 0),
                  pl.BlockSpec((1, BRANCH_WIDTH, tm), lambda bi, si: (bi, 0, si)),
                  tok(BRANCH_WIDTH, 0), tok(BRANCH_WIDTH, 0),
                  tok(d, COL_GATES // d), tok(d, COL_GATES // d + 1), tok(d, COL_GATES // d + 2),
                  pl.BlockSpec((N_BRANCH, BRANCH_WIDTH, d), lambda bi, si: (0, 0, 0)),
                  pl.BlockSpec((d, d), lambda bi, si: (0, 0))],
        out_specs=tok(d, 0),
        out_shape=jax.ShapeDtypeStruct((b, s, d), F32),
        compiler_params=_cparams("parallel", "parallel"),
        name="merge_out_proj",
    )(x, yaT, yb, yc, proj, proj, proj, w_branch.astype(BF16), w_out.astype(BF16))


def _ffn_body(x_ref, g_ref, wg_ref, wu_ref, wd_ref, o_ref, h_ref, acc_ref):
    f = pl.program_id(1)

    @pl.when(f == 0)
    def _():
        h_ref[...] = (_rms(x_ref[...]) * g_ref[...]).astype(BF16)
        acc_ref[...] = jnp.zeros_like(acc_ref)

    h = h_ref[...]
    a = jnp.dot(h, wg_ref[...], preferred_element_type=F32)
    u = jnp.dot(h, wu_ref[...], preferred_element_type=F32)
    act = (a * jax.nn.sigmoid(a) * u).astype(BF16)
    acc_ref[...] += jnp.dot(act, wd_ref[...], preferred_element_type=F32)

    @pl.when(f == pl.num_programs(1) - 1)
    def _():
        o_ref[...] = x_ref[...] + acc_ref[...]


def _ffn(x2d, g, w_gate, w_up, w_down):
    t, d = x2d.shape
    ff = w_gate.shape[1]
    tm, tf = _tile(t, TM_FFN), _tile(ff, TF_FFN)
    return pl.pallas_call(
        _ffn_body,
        grid=(t // tm, ff // tf),
        in_specs=[pl.BlockSpec((tm, d), lambda i, f: (i, 0)),
                  pl.BlockSpec((1, d), lambda i, f: (0, 0)),
                  pl.BlockSpec((d, tf), lambda i, f: (0, f)),
                  pl.BlockSpec((d, tf), lambda i, f: (0, f)),
                  pl.BlockSpec((tf, d), lambda i, f: (f, 0))],
        out_specs=pl.BlockSpec((tm, d), lambda i, f: (i, 0)),
        out_shape=jax.ShapeDtypeStruct((t, d), F32),
        scratch_shapes=[pltpu.VMEM((tm, d), BF16), pltpu.VMEM((tm, d), F32)],
        compiler_params=_cparams("parallel", "arbitrary"),
        name="ffn_dense",
    )(x2d, g.reshape(1, d), w_gate.astype(BF16), w_up.astype(BF16), w_down.astype(BF16))


def _moe_body(x_ref, g_ref, wr_ref, wg_ref, wu_ref, wd_ref, o_ref, h_ref, comb_ref, acc_ref):
    e = pl.program_id(1)
    lane = lax.broadcasted_iota(jnp.int32, comb_ref.shape, 1)

    @pl.when(e == 0)
    def _():
        hf = _rms(x_ref[...]) * g_ref[...]
        h_ref[...] = hf.astype(BF16)
        acc_ref[...] = jnp.zeros_like(acc_ref)
        logits = jnp.dot(hf, wr_ref[...], precision=HIGHEST, preferred_element_type=F32)
        m1 = jnp.max(logits, axis=-1, keepdims=True)
        i1 = jnp.min(jnp.where(logits == m1, lane, N_EXPERTS), axis=-1, keepdims=True)
        rest = jnp.where(lane == i1, -jnp.inf, logits)
        m2 = jnp.max(rest, axis=-1, keepdims=True)
        i2 = jnp.min(jnp.where(rest == m2, lane, N_EXPERTS), axis=-1, keepdims=True)
        e2 = jnp.exp(m2 - m1)
        w1 = 1.0 / (1.0 + e2)
        comb_ref[...] = jnp.where(lane == i1, w1, 0.0) + jnp.where(lane == i2, e2 * w1, 0.0)

    c = jnp.sum(jnp.where(lane == e, comb_ref[...], 0.0), axis=-1, keepdims=True)
    h = h_ref[...]
    a = jnp.dot(h, wg_ref[0], preferred_element_type=F32)
    u = jnp.dot(h, wu_ref[0], preferred_element_type=F32)
    act = (a * jax.nn.sigmoid(a) * u * c).astype(BF16)
    acc_ref[...] += jnp.dot(act, wd_ref[0], preferred_element_type=F32)

    @pl.when(e == pl.num_programs(1) - 1)
    def _():
        o_ref[...] = x_ref[...] + acc_ref[...]


def _moe(x2d, g, w_router, w_gate, w_up, w_down):
    t, d = x2d.shape
    ne, _, fe = w_gate.shape
    tm = _tile(t, TM_MOE)
    return pl.pallas_call(
        _moe_body,
        grid=(t // tm, ne),
        in_specs=[pl.BlockSpec((tm, d), lambda i, e: (i, 0)),
                  pl.BlockSpec((1, d), lambda i, e: (0, 0)),
                  pl.BlockSpec((d, ne), lambda i, e: (0, 0)),
                  pl.BlockSpec((1, d, fe), lambda i, e: (e, 0, 0)),
                  pl.BlockSpec((1, d, fe), lambda i, e: (e, 0, 0)),
                  pl.BlockSpec((1, fe, d), lambda i, e: (e, 0, 0))],
        out_specs=pl.BlockSpec((tm, d), lambda i, e: (i, 0)),
        out_shape=jax.ShapeDtypeStruct((t, d), F32),
        scratch_shapes=[pltpu.VMEM((tm, d), BF16), pltpu.VMEM((tm, ne), F32), pltpu.VMEM((tm, d), F32)],
        compiler_params=_cparams("parallel", "arbitrary"),
        name="moe_dense",
    )(x2d, g.reshape(1, d), w_router, w_gate.astype(BF16), w_up.astype(BF16), w_down.astype(BF16))


def _permute_w_in(w):
    off = np.concatenate([[0], np.cumsum(IN_SPLITS)]).tolist()
    piece = lambda i: w[:, off[i]:off[i + 1]]
    order = [9, 0, 1, 3, 4, 5, 7, 8, 2, 6]
    pad = jnp.zeros((w.shape[0], D_PROJ - off[-1]), w.dtype)
    return jnp.concatenate([piece(i) for i in order] + [pad], axis=1).astype(BF16)


def kernel(x, positions, g_mix, w_in, g_cq, w_uq, g_ckv, w_ukv, g_qk_q, g_qk_k, w_gla_gate_up, b_gla_gate, g_gla_out, w_pool, pool_scale, w_branch, w_out, g_ffn, w_ffn_gate, w_ffn_up, w_ffn_down, w_router, w_exp_gate, w_exp_up, w_exp_down):
    b, s, d = x.shape
    depth = g_mix.shape[0]
    for layer in range(depth):
        proj = _norm_matmul(x.reshape(b * s, d), g_mix[layer], _permute_w_in(w_in[layer]), BF16).reshape(b, s, D_PROJ)
        qT, k, vT = _mla_prep(proj, positions, g_cq[layer], w_uq[layer], g_ckv[layer], w_ukv[layer],
                              g_qk_q[layer], g_qk_k[layer])
        yaT = _attention(qT, k, vT).reshape(b, MLA_HEADS * MLA_V, s)
        yb = _gla(proj, w_gla_gate_up[layer], b_gla_gate[layer], g_gla_out[layer])
        yc = _pool(proj, w_pool[layer], pool_scale[layer])
        x = _merge(x, yaT, yb, yc, proj, w_branch[layer], w_out[layer])
        i = layer // 2
        if layer % 2 == 0:
            x2d = _ffn(x.reshape(b * s, d), g_ffn[layer], w_ffn_gate[i], w_ffn_up[i], w_ffn_down[i])
        else:
            x2d = _moe(x.reshape(b * s, d), g_ffn[layer], w_router[i], w_exp_gate[i], w_exp_up[i], w_exp_down[i])
        x = x2d.reshape(b, s, d)
    return x
```

```python
import functools

import jax
import jax.numpy as jnp
import numpy as np
from jax import lax
from jax.experimental import pallas as pl
from jax.experimental.pallas import tpu as pltpu

F32 = jnp.float32
BF16 = jnp.bfloat16
HIGHEST = lax.Precision.HIGHEST

D_MODEL = 1024
MLA_HEADS = 8
MLA_NOPE = 64
MLA_ROPE = 32
MLA_QK = MLA_NOPE + MLA_ROPE
MLA_V = 64
MLA_Q_RANK = D_MODEL // 4
MLA_KV_RANK = D_MODEL // 4
ROPE_BASE = 10000.0
GLA_HEADS = 4
GLA_DK = 64
GLA_DV = 128
GLA_GATE_RANK = 16
GLA_TAU = 16.0
POOL_WINDOWS = (2, 4, 8, 16)
POOL_GROUP = 128
POOL_WIDTH = 4 * POOL_GROUP
N_BRANCH = 3
BRANCH_WIDTH = 512
D_FF = 2816
N_EXPERTS = 8
TOP_K = 2
D_EXPERT = 1408
EPS = 1e-6

IN_SPLITS = (MLA_Q_RANK, MLA_KV_RANK, MLA_ROPE,
             GLA_HEADS * GLA_DK, GLA_HEADS * GLA_DK, GLA_HEADS * GLA_DV, GLA_GATE_RANK, GLA_HEADS * GLA_DV,
             POOL_WIDTH, N_BRANCH * D_MODEL)

LANES = 128
HEAD_PAD = LANES

COL_GATES = 0
COL_CQ = 3072
COL_CKV = 3328
COL_GQ = 3584
COL_GK = 3840
COL_GV = 4096
COL_GOG = 4608
COL_POOL = 5120
COL_SMALL = 5632
D_PROJ = 5760
SMALL_LR = MLA_ROPE

TM_PROJ = 1024
TN_PROJ = 1920
ATT_T = 512
ATT_NH = 4
V_ROWS = 80
GLA_L = 128
GLA_BASE = 1
TM_POOL = 256
POOL_HALO = 16
TM_MERGE = 512
TM_FFN = 1024
TF_FFN = 256
TM_MOE = 512
VMEM_LIMIT = 56 * 1024 * 1024


def _cparams(*sem):
    return pltpu.CompilerParams(dimension_semantics=sem, vmem_limit_bytes=VMEM_LIMIT)


def _tile(n, pref):
    t = min(n, pref)
    assert n % t == 0, (n, pref)
    return t


def _rms(x):
    return x * lax.rsqrt(jnp.mean(x * x, axis=-1, keepdims=True) + EPS)


_NT = (((1,), (1,)), ((), ()))
_TN = (((0,), (0,)), ((), ()))


def _norm_matmul_body(x_ref, g_ref, w_ref, o_ref, h_ref):
    @pl.when(pl.program_id(1) == 0)
    def _():
        h_ref[...] = (_rms(x_ref[...]) * g_ref[...]).astype(BF16)

    o_ref[...] = jnp.dot(h_ref[...], w_ref[...], preferred_element_type=F32).astype(o_ref.dtype)


def _norm_matmul(x2d, g, w, out_dtype):
    t, d = x2d.shape
    n = w.shape[1]
    tm, tn = _tile(t, TM_PROJ), _tile(n, TN_PROJ)
    return pl.pallas_call(
        _norm_matmul_body,
        grid=(t // tm, n // tn),
        in_specs=[pl.BlockSpec((tm, d), lambda i, j: (i, 0)),
                  pl.BlockSpec((1, d), lambda i, j: (0, 0)),
                  pl.BlockSpec((d, tn), lambda i, j: (0, j))],
        out_specs=pl.BlockSpec((tm, tn), lambda i, j: (i, j)),
        out_shape=jax.ShapeDtypeStruct((t, n), out_dtype),
        scratch_shapes=[pltpu.VMEM((tm, d), BF16)],
        compiler_params=_cparams("parallel", "arbitrary"),
        name="norm_in_proj",
    )(x2d, g.reshape(1, d), w)


def _mla_prep_body(pos_ref, cq_ref, ckv_ref, small_ref, gcq_ref, wuq_ref, gckv_ref, wukv_ref, gq_ref, gk_ref,
                   invf_ref, qT_ref, k_ref, vT_ref):
    tp = cq_ref.shape[1]
    half = MLA_ROPE // 2
    ang = invf_ref[...] * pos_ref[0].astype(F32)
    cos, sin = jnp.cos(ang), jnp.sin(ang)

    def rope(t):
        x1, x2 = t[:half], t[half:]
        return x1 * cos - x2 * sin, x2 * cos + x1 * sin

    cqn = (_rms(cq_ref[0].astype(F32)) * gcq_ref[...]).astype(BF16)
    ckvn = (_rms(ckv_ref[0].astype(F32)) * gckv_ref[...]).astype(BF16)
    qT = lax.dot_general(wuq_ref[...], cqn, _NT, preferred_element_type=F32)
    kvT = lax.dot_general(wukv_ref[...], ckvn, _NT, preferred_element_type=F32)
    kr = small_ref[0].astype(F32).T[:MLA_ROPE]
    kr_ss = jnp.sum(kr * kr, axis=0, keepdims=True)
    gq, gk = gq_ref[...], gk_ref[...]
    zpad = jnp.zeros((HEAD_PAD - MLA_QK, tp), F32)
    vrow = lax.broadcasted_iota(jnp.int32, (V_ROWS - MLA_V, tp), 0)
    vpad = jnp.where(vrow == 0, 1.0, 0.0).astype(F32)
    scale = MLA_QK ** -0.5 * np.log2(np.e)
    for h in range(MLA_HEADS):
        qh = qT[h * MLA_QK:(h + 1) * MLA_QK]
        qn = qh * lax.rsqrt(jnp.mean(qh * qh, axis=0, keepdims=True) + EPS) * (gq * scale)
        r1, r2 = rope(qn[MLA_NOPE:])
        qT_ref[0, h] = jnp.concatenate([qn[:MLA_NOPE], r1, r2, zpad], axis=0).astype(BF16)
        base = h * (MLA_NOPE + MLA_V)
        kn = kvT[base:base + MLA_NOPE]
        r = lax.rsqrt((jnp.sum(kn * kn, axis=0, keepdims=True) + kr_ss) * (1.0 / MLA_QK) + EPS)
        r1, r2 = rope(kr * r * gk[MLA_NOPE:])
        kfm = jnp.concatenate([kn * r * gk[:MLA_NOPE], r1, r2, zpad], axis=0)
        k_ref[0, h] = kfm.T.astype(BF16)
        vT_ref[0, h, 0] = jnp.concatenate([kvT[base + MLA_NOPE:base + MLA_NOPE + MLA_V], vpad], axis=0).astype(BF16)


def _mla_prep(proj, positions, g_cq, w_uq, g_ckv, w_ukv, g_qk_q, g_qk_k):
    b, s, _ = proj.shape
    tp = _tile(s, ATT_T)
    ns = s // tp
    h = MLA_HEADS
    pos = positions.reshape(b * ns, 1, tp)
    inv_freq = (ROPE_BASE ** (-jnp.arange(MLA_ROPE // 2, dtype=F32) / (MLA_ROPE // 2))).reshape(-1, 1)
    const = lambda shape: pl.BlockSpec(shape, lambda bi, si: (0,) * len(shape))
    return pl.pallas_call(
        _mla_prep_body,
        grid=(b, ns),
        in_specs=[pl.BlockSpec((1, 1, tp), lambda bi, si: (bi * ns + si, 0, 0)),
                  pl.BlockSpec((1, tp, MLA_Q_RANK), lambda bi, si: (bi, si, COL_CQ // MLA_Q_RANK)),
                  pl.BlockSpec((1, tp, MLA_KV_RANK), lambda bi, si: (bi, si, COL_CKV // MLA_KV_RANK)),
                  pl.BlockSpec((1, tp, LANES), lambda bi, si: (bi, si, COL_SMALL // LANES)),
                  const((1, MLA_Q_RANK)), const((h * MLA_QK, MLA_Q_RANK)),
                  const((1, MLA_KV_RANK)), const((h * (MLA_NOPE + MLA_V), MLA_KV_RANK)),
                  const((MLA_QK, 1)), const((MLA_QK, 1)), const((MLA_ROPE // 2, 1))],
        out_specs=[pl.BlockSpec((1, h, HEAD_PAD, tp), lambda bi, si: (bi, 0, 0, si)),
                   pl.BlockSpec((1, h, tp, HEAD_PAD), lambda bi, si: (bi, 0, si, 0)),
                   pl.BlockSpec((1, h, 1, V_ROWS, tp), lambda bi, si: (bi, 0, si, 0, 0))],
        out_shape=[jax.ShapeDtypeStruct((b, h, HEAD_PAD, s), BF16),
                   jax.ShapeDtypeStruct((b, h, s, HEAD_PAD), BF16),
                   jax.ShapeDtypeStruct((b, h, ns, V_ROWS, tp), BF16)],
        compiler_params=_cparams("parallel", "parallel"),
        name="mla_prep",
    )(pos, proj, proj, proj, g_cq.reshape(1, -1), w_uq.T.astype(BF16), g_ckv.reshape(1, -1), w_ukv.T.astype(BF16),
      g_qk_q.reshape(-1, 1), g_qk_k.reshape(-1, 1), inv_freq)


def _attn_body(qT_ref, k_ref, vT_ref, o_ref, *, tq, nh):
    i = pl.program_id(2)

    def step(j, carry, masked):
        scores = []
        for h in range(nh):
            kb = k_ref[0, h, pl.ds(pl.multiple_of(j * tq, tq), tq), :]
            s = jnp.dot(kb, qT_ref[0, h], preferred_element_type=F32)
            if masked:
                kpos = lax.broadcasted_iota(jnp.int32, (tq, tq), 0)
                qpos = lax.broadcasted_iota(jnp.int32, (tq, tq), 1)
                s = jnp.where(kpos <= qpos, s, -jnp.inf)
            scores.append(s)
        out = []
        for h in range(nh):
            m, acc = carry[h]
            m_new = jnp.maximum(m, jnp.max(scores[h], axis=0, keepdims=True))
            p = jnp.exp2(scores[h] - m_new).astype(BF16)
            acc = jnp.exp2(m - m_new) * acc + jnp.dot(vT_ref[0, h, j], p, preferred_element_type=F32)
            out.append((m_new, acc))
        return tuple(out)

    carry = tuple((jnp.full((1, tq), -jnp.inf, F32), jnp.zeros((V_ROWS, tq), F32)) for _ in range(nh))
    carry = lax.fori_loop(0, i, functools.partial(step, masked=False), carry)
    carry = step(i, carry, True)
    for h in range(nh):
        acc = carry[h][1]
        o_ref[0, h] = (acc[:MLA_V] / acc[MLA_V:MLA_V + 1]).astype(o_ref.dtype)


def _attention(qT, k, vT):
    b, h, _, s = qT.shape
    tq = vT.shape[-1]
    nk = s // tq
    nh = ATT_NH
    return pl.pallas_call(
        functools.partial(_attn_body, tq=tq, nh=nh),
        grid=(b, h // nh, s // tq),
        in_specs=[pl.BlockSpec((1, nh, HEAD_PAD, tq), lambda bi, hi, qi: (bi, hi, 0, qi)),
                  pl.BlockSpec((1, nh, s, HEAD_PAD), lambda bi, hi, qi: (bi, hi, 0, 0)),
                  pl.BlockSpec((1, nh, nk, V_ROWS, tq), lambda bi, hi, qi: (bi, hi, 0, 0, 0))],
        out_specs=pl.BlockSpec((1, nh, MLA_V, tq), lambda bi, hi, qi: (bi, hi, 0, qi)),
        out_shape=jax.ShapeDtypeStruct((b, h, MLA_V, s), BF16),
        compiler_params=_cparams("parallel", "parallel", "arbitrary"),
        name="mla_attention",
    )(qT, k, vT)


def _gla_body(q_ref, k_ref, v_ref, og_ref, small_ref, wg_ref, bg_ref, gout_ref, o_ref, st_ref, *, blk):
    @pl.when(pl.program_id(1) == 0)
    def _():
        st_ref[...] = jnp.zeros_like(st_ref)

    hd = GLA_HEADS * GLA_DK
    lr = small_ref[0][:, SMALL_LR:SMALL_LR + GLA_GATE_RANK].astype(F32)
    z = jnp.dot(lr, wg_ref[...], precision=HIGHEST, preferred_element_type=F32) + bg_ref[...]
    log_a = jax.nn.log_sigmoid(z) * (1.0 / GLA_TAU)
    row = lax.broadcasted_iota(jnp.int32, (blk, blk), 0)
    col = lax.broadcasted_iota(jnp.int32, (blk, blk), 1)
    bc = jnp.dot((col <= row).astype(F32), log_a, precision=HIGHEST, preferred_element_type=F32)
    q = q_ref[0].astype(F32) * (GLA_DK ** -0.5)
    k = k_ref[0].astype(F32)

    def ref_rows(group, off):
        parts = [jnp.broadcast_to(bc[g0 + off:g0 + off + 1], (group, hd)) for g0 in range(0, blk, group)]
        return parts[0] if len(parts) == 1 else jnp.concatenate(parts, axis=0)

    levels = []
    group = blk
    while group > GLA_BASE:
        half = group // 2
        ref = ref_rows(group, half - 1)
        qs = (q * jnp.exp(jnp.minimum(bc - ref, 0.0))).astype(BF16)
        ks = (k * jnp.exp(jnp.minimum(ref - bc, 0.0))).astype(BF16)
        mask = ((row & -group) == (col & -group)) & ((row & half) != 0) & ((col & half) == 0)
        levels.append((qs, ks, mask))
        group = half
    if GLA_BASE == 1:
        levels.append((q.astype(BF16), k.astype(BF16), row == col))
    else:
        ref = ref_rows(GLA_BASE, GLA_BASE // 2 - 1)
        mask = ((row & -GLA_BASE) == (col & -GLA_BASE)) & (col <= row)
        levels.append(((q * jnp.exp(bc - ref)).astype(BF16), (k * jnp.exp(ref - bc)).astype(BF16), mask))

    qd = (q * jnp.exp(bc)).astype(BF16)
    b_last = bc[blk - 1:blk]
    kd = (k * jnp.exp(b_last - bc)).astype(BF16)
    a_last = jnp.exp(b_last)
    gout = gout_ref[...]
    for h in range(GLA_HEADS):
        ks_, vs_ = slice(h * GLA_DK, (h + 1) * GLA_DK), slice(h * GLA_DV, (h + 1) * GLA_DV)
        attn = jnp.zeros((blk, blk), F32)
        for qs, ksc, mask in levels:
            p = lax.dot_general(qs[:, ks_], ksc[:, ks_], _NT, preferred_element_type=F32)
            attn = jnp.where(mask, p, attn)
        v_h = v_ref[0, :, vs_]
        st = st_ref[h]
        o = jnp.dot(attn.astype(BF16), v_h, preferred_element_type=F32)
        o = o + lax.dot_general(qd[:, ks_], st.astype(BF16), _NT, preferred_element_type=F32)
        st_ref[h] = st * a_last[:, ks_] + lax.dot_general(v_h, kd[:, ks_], _TN, preferred_element_type=F32)
        og = og_ref[0, :, vs_].astype(F32)
        o_ref[0, :, vs_] = (_rms(o) * gout * (og * jax.nn.sigmoid(og))).astype(o_ref.dtype)


def _gla(proj, w_gate_up, b_gate, g_out):
    b, s, _ = proj.shape
    blk = _tile(s, GLA_L)
    hd, hv = GLA_HEADS * GLA_DK, GLA_HEADS * GLA_DV
    const = lambda shape: pl.BlockSpec(shape, lambda bi, si: (0,) * len(shape))
    return pl.pallas_call(
        functools.partial(_gla_body, blk=blk),
        grid=(b, s // blk),
        in_specs=[pl.BlockSpec((1, blk, hd), lambda bi, si: (bi, si, COL_GQ // hd)),
                  pl.BlockSpec((1, blk, hd), lambda bi, si: (bi, si, COL_GK // hd)),
                  pl.BlockSpec((1, blk, hv), lambda bi, si: (bi, si, COL_GV // hv)),
                  pl.BlockSpec((1, blk, hv), lambda bi, si: (bi, si, COL_GOG // hv)),
                  pl.BlockSpec((1, blk, LANES), lambda bi, si: (bi, si, COL_SMALL // LANES)),
                  const((GLA_GATE_RANK, hd)), const((1, hd)), const((1, GLA_DV))],
        out_specs=pl.BlockSpec((1, blk, hv), lambda bi, si: (bi, si, 0)),
        out_shape=jax.ShapeDtypeStruct((b, s, hv), BF16),
        scratch_shapes=[pltpu.VMEM((GLA_HEADS, GLA_DV, GLA_DK), F32)],
        compiler_params=_cparams("parallel", "arbitrary"),
        name="gla",
    )(proj, proj, proj, proj, proj, w_gate_up, b_gate.reshape(1, -1), g_out.reshape(1, -1))


def _pool_body(u_ref, halo_ref, w_ref, scale_ref, o_ref, *, tm):
    si = pl.program_id(1)
    row = lax.broadcasted_iota(jnp.int32, (tm, tm + POOL_HALO), 0)
    col = lax.broadcasted_iota(jnp.int32, (tm, tm + POOL_HALO), 1)
    t1 = (si * tm + lax.broadcasted_iota(jnp.int32, (tm, 1), 0) + 1).astype(F32)
    for g, win in enumerate(POOL_WINDOWS):
        cs = slice(g * POOL_GROUP, (g + 1) * POOL_GROUP)
        cur = u_ref[0, :, cs]
        halo = halo_ref[0, :, cs]
        ext = jnp.concatenate([jnp.where(si > 0, halo, jnp.zeros_like(halo)), cur], axis=0)
        band = ((col <= row + POOL_HALO) & (col > row + POOL_HALO - win)).astype(BF16)
        wsum = jnp.dot(band, ext, preferred_element_type=F32)
        pooled = wsum / jnp.minimum(t1, float(win)) - cur.astype(F32)
        y = jnp.dot(pooled.astype(BF16), w_ref[g], preferred_element_type=F32)
        o_ref[0, :, cs] = (y * scale_ref[:, cs]).astype(o_ref.dtype)


def _pool(proj, w_pool, pool_scale):
    b, s, _ = proj.shape
    tm = _tile(s, TM_POOL)
    hb = tm // POOL_HALO
    return pl.pallas_call(
        functools.partial(_pool_body, tm=tm),
        grid=(b, s // tm),
        in_specs=[pl.BlockSpec((1, tm, POOL_WIDTH), lambda bi, si: (bi, si, COL_POOL // POOL_WIDTH)),
                  pl.BlockSpec((1, POOL_HALO, POOL_WIDTH),
                               lambda bi, si: (bi, jnp.maximum(si * hb - 1, 0), COL_POOL // POOL_WIDTH)),
                  pl.BlockSpec((len(POOL_WINDOWS), POOL_GROUP, POOL_GROUP), lambda bi, si: (0, 0, 0)),
                  pl.BlockSpec((1, POOL_WIDTH), lambda bi, si: (0, 0))],
        out_specs=pl.BlockSpec((1, tm, POOL_WIDTH), lambda bi, si: (bi, si, 0)),
        out_shape=jax.ShapeDtypeStruct((b, s, POOL_WIDTH), BF16),
        compiler_params=_cparams("parallel", "parallel"),
        name="pool",
    )(proj, proj, w_pool.astype(BF16), pool_scale.reshape(1, -1))


def _merge_body(x_ref, yaT_ref, yb_ref, yc_ref, g0_ref, g1_ref, g2_ref, wb_ref, wo_ref, o_ref):
    d_a = lax.dot_general(yaT_ref[0], wb_ref[0], _TN, preferred_element_type=F32)
    d_b = jnp.dot(yb_ref[0], wb_ref[1], preferred_element_type=F32)
    d_c = jnp.dot(yc_ref[0], wb_ref[2], preferred_element_type=F32)
    merged = (jax.nn.sigmoid(g0_ref[0].astype(F32)) * d_a + jax.nn.sigmoid(g1_ref[0].astype(F32)) * d_b
              + jax.nn.sigmoid(g2_ref[0].astype(F32)) * d_c)
    o_ref[0] = x_ref[0] + jnp.dot(merged.astype(BF16), wo_ref[...], preferred_element_type=F32)


def _merge(x, yaT, yb, yc, proj, w_branch, w_out):
    b, s, d = x.shape
    tm = _tile(s, TM_MERGE)
    tok = lambda width, cb: pl.BlockSpec((1, tm, width), lambda bi, si: (bi, si, cb))
    return pl.pallas_call(
        _merge_body,
        grid=(b, s // tm),
        in_specs=[tok(d, 0),
                  pl.BlockSpec((1, BRANCH_WIDTH, tm), lambda bi, si: (bi, 0, si)),
                  tok(BRANCH_WIDTH, 0), tok(BRANCH_WIDTH, 0),
                  tok(d, COL_GATES // d), tok(d, COL_GATES // d + 1), tok(d, COL_GATES // d + 2),
                  pl.BlockSpec((N_BRANCH, BRANCH_WIDTH, d), lambda bi, si: (0, 0, 0)),
                  pl.BlockSpec((d, d), lambda bi, si: (0, 0))],
        out_specs=tok(d, 0),
        out_shape=jax.ShapeDtypeStruct((b, s, d), F32),
        compiler_params=_cparams("parallel", "parallel"),
        name="merge_out_proj",
    )(x, yaT, yb, yc, proj, proj, proj, w_branch.astype(BF16), w_out.astype(BF16))


def _ffn_body(x_ref, g_ref, wg_ref, wu_ref, wd_ref, o_ref, h_ref, acc_ref):
    f = pl.program_id(1)

    @pl.when(f == 0)
    def _():
        h_ref[...] = (_rms(x_ref[...]) * g_ref[...]).astype(BF16)
        acc_ref[...] = jnp.zeros_like(acc_ref)

    h = h_ref[...]
    a = jnp.dot(h, wg_ref[...], preferred_element_type=F32)
    u = jnp.dot(h, wu_ref[...], preferred_element_type=F32)
    act = (a * jax.nn.sigmoid(a) * u).astype(BF16)
    acc_ref[...] += jnp.dot(act, wd_ref[...], preferred_element_type=F32)

    @pl.when(f == pl.num_programs(1) - 1)
    def _():
        o_ref[...] = x_ref[...] + acc_ref[...]


def _ffn(x2d, g, w_gate, w_up, w_down):
    t, d = x2d.shape
    ff = w_gate.shape[1]
    tm, tf = _tile(t, TM_FFN), _tile(ff, TF_FFN)
    return pl.pallas_call(
        _ffn_body,
        grid=(t // tm, ff // tf),
        in_specs=[pl.BlockSpec((tm, d), lambda i, f: (i, 0)),
                  pl.BlockSpec((1, d), lambda i, f: (0, 0)),
                  pl.BlockSpec((d, tf), lambda i, f: (0, f)),
                  pl.BlockSpec((d, tf), lambda i, f: (0, f)),
                  pl.BlockSpec((tf, d), lambda i, f: (f, 0))],
        out_specs=pl.BlockSpec((tm, d), lambda i, f: (i, 0)),
        out_shape=jax.ShapeDtypeStruct((t, d), F32),
        scratch_shapes=[pltpu.VMEM((tm, d), BF16), pltpu.VMEM((tm, d), F32)],
        compiler_params=_cparams("parallel", "arbitrary"),
        name="ffn_dense",
    )(x2d, g.reshape(1, d), w_gate.astype(BF16), w_up.astype(BF16), w_down.astype(BF16))


def _moe_body(x_ref, g_ref, wr_ref, wg_ref, wu_ref, wd_ref, o_ref, h_ref, comb_ref, acc_ref):
    e = pl.program_id(1)
    lane = lax.broadcasted_iota(jnp.int32, comb_ref.shape, 1)

    @pl.when(e == 0)
    def _():
        hf = _rms(x_ref[...]) * g_ref[...]
        h_ref[...] = hf.astype(BF16)
        acc_ref[...] = jnp.zeros_like(acc_ref)
        logits = jnp.dot(hf, wr_ref[...], precision=HIGHEST, preferred_element_type=F32)
        m1 = jnp.max(logits, axis=-1, keepdims=True)
        i1 = jnp.min(jnp.where(logits == m1, lane, N_EXPERTS), axis=-1, keepdims=True)
        rest = jnp.where(lane == i1, -jnp.inf, logits)
        m2 = jnp.max(rest, axis=-1, keepdims=True)
        i2 = jnp.min(jnp.where(rest == m2, lane, N_EXPERTS), axis=-1, keepdims=True)
        e2 = jnp.exp(m2 - m1)
        w1 = 1.0 / (1.0 + e2)
        comb_ref[...] = jnp.where(lane == i1, w1, 0.0) + jnp.where(lane == i2, e2 * w1, 0.0)

    c = jnp.sum(jnp.where(lane == e, comb_ref[...], 0.0), axis=-1, keepdims=True)
    h = h_ref[...]
    a = jnp.dot(h, wg_ref[0], preferred_element_type=F32)
    u = jnp.dot(h, wu_ref[0], preferred_element_type=F32)
    act = (a * jax.nn.sigmoid(a) * u * c).astype(BF16)
    acc_ref[...] += jnp.dot(act, wd_ref[0], preferred_element_type=F32)

    @pl.when(e == pl.num_programs(1) - 1)
    def _():
        o_ref[...] = x_ref[...] + acc_ref[...]


def _moe(x2d, g, w_router, w_gate, w_up, w_down):
    t, d = x2d.shape
    ne, _, fe = w_gate.shape
    tm = _tile(t, TM_MOE)
    return pl.pallas_call(
        _moe_body,
        grid=(t // tm, ne),
        in_specs=[pl.BlockSpec((tm, d), lambda i, e: (i, 0)),
                  pl.BlockSpec((1, d), lambda i, e: (0, 0)),
                  pl.BlockSpec((d, ne), lambda i, e: (0, 0)),
                  pl.BlockSpec((1, d, fe), lambda i, e: (e, 0, 0)),
                  pl.BlockSpec((1, d, fe), lambda i, e: (e, 0, 0)),
                  pl.BlockSpec((1, fe, d), lambda i, e: (e, 0, 0))],
        out_specs=pl.BlockSpec((tm, d), lambda i, e: (i, 0)),
        out_shape=jax.ShapeDtypeStruct((t, d), F32),
        scratch_shapes=[pltpu.VMEM((tm, d), BF16), pltpu.VMEM((tm, ne), F32), pltpu.VMEM((tm, d), F32)],
        compiler_params=_cparams("parallel", "arbitrary"),
        name="moe_dense",
    )(x2d, g.reshape(1, d), w_router, w_gate.astype(BF16), w_up.astype(BF16), w_down.astype(BF16))


def _permute_w_in(w):
    off = np.concatenate([[0], np.cumsum(IN_SPLITS)]).tolist()
    piece = lambda i: w[:, off[i]:off[i + 1]]
    order = [9, 0, 1, 3, 4, 5, 7, 8, 2, 6]
    pad = jnp.zeros((w.shape[0], D_PROJ - off[-1]), w.dtype)
    return jnp.concatenate([piece(i) for i in order] + [pad], axis=1).astype(BF16)


def kernel(x, positions, g_mix, w_in, g_cq, w_uq, g_ckv, w_ukv, g_qk_q, g_qk_k, w_gla_gate_up, b_gla_gate, g_gla_out, w_pool, pool_scale, w_branch, w_out, g_ffn, w_ffn_gate, w_ffn_up, w_ffn_down, w_router, w_exp_gate, w_exp_up, w_exp_down):
    b, s, d = x.shape
    depth = g_mix.shape[0]
    for layer in range(depth):
        proj = _norm_matmul(x.reshape(b * s, d), g_mix[layer], _permute_w_in(w_in[layer]), BF16).reshape(b, s, D_PROJ)
        qT, k, vT = _mla_prep(proj, positions, g_cq[layer], w_uq[layer], g_ckv[layer], w_ukv[layer],
                              g_qk_q[layer], g_qk_k[layer])
        yaT = _attention(qT, k, vT).reshape(b, MLA_HEADS * MLA_V, s)
        yb = _gla(proj, w_gla_gate_up[layer], b_gla_gate[layer], g_gla_out[layer])
        yc = _pool(proj, w_pool[layer], pool_scale[layer])
        x = _merge(x, yaT, yb, yc, proj, w_branch[layer], w_out[layer])
        i = layer // 2
        if layer % 2 == 0:
            x2d = _ffn(x.reshape(b * s, d), g_ffn[layer], w_ffn_gate[i], w_ffn_up[i], w_ffn_down[i])
        else:
            x2d = _moe(x.reshape(b * s, d), g_ffn[layer], w_router[i], w_exp_gate[i], w_exp_up[i], w_exp_down[i])
        x = x2d.reshape(b, s, d)
    return x
```

```python
import functools

import jax
import jax.numpy as jnp
import numpy as np
from jax import lax
from jax.experimental import pallas as pl
from jax.experimental.pallas import tpu as pltpu

F32 = jnp.float32
BF16 = jnp.bfloat16
HIGHEST = lax.Precision.HIGHEST

D_MODEL = 1024
MLA_HEADS = 8
MLA_NOPE = 64
MLA_ROPE = 32
MLA_QK = MLA_NOPE + MLA_ROPE
MLA_V = 64
MLA_Q_RANK = D_MODEL // 4
MLA_KV_RANK = D_MODEL // 4
ROPE_BASE = 10000.0
GLA_HEADS = 4
GLA_DK = 64
GLA_DV = 128
GLA_GATE_RANK = 16
GLA_TAU = 16.0
POOL_WINDOWS = (2, 4, 8, 16)
POOL_GROUP = 128
POOL_WIDTH = 4 * POOL_GROUP
N_BRANCH = 3
BRANCH_WIDTH = 512
D_FF = 2816
N_EXPERTS = 8
TOP_K = 2
D_EXPERT = 1408
EPS = 1e-6

IN_SPLITS = (MLA_Q_RANK, MLA_KV_RANK, MLA_ROPE,
             GLA_HEADS * GLA_DK, GLA_HEADS * GLA_DK, GLA_HEADS * GLA_DV, GLA_GATE_RANK, GLA_HEADS * GLA_DV,
             POOL_WIDTH, N_BRANCH * D_MODEL)

LANES = 128
HEAD_PAD = LANES

COL_GATES = 0
COL_CQ = 3072
COL_CKV = 3328
COL_GQ = 3584
COL_GK = 3840
COL_GV = 4096
COL_GOG = 4608
COL_POOL = 5120
COL_SMALL = 5632
D_PROJ = 5760
SMALL_LR = MLA_ROPE

TM_PROJ = 1024
TN_PROJ = 1920
ATT_T = 512
ATT_NH = 4
V_ROWS = 80
GLA_L = 128
GLA_BASE = 1
TM_POOL = 256
POOL_HALO = 16
TM_MERGE = 512
TM_FFN = 1024
TF_FFN = 256
TM_ROUTE = 1024
TM_MOE = 512
TM_COMBINE = 512
VMEM_LIMIT = 56 * 1024 * 1024


def _cparams(*sem):
    return pltpu.CompilerParams(dimension_semantics=sem, vmem_limit_bytes=VMEM_LIMIT)


def _tile(n, pref):
    t = min(n, pref)
    assert n % t == 0, (n, pref)
    return t


def _rms(x):
    return x * lax.rsqrt(jnp.mean(x * x, axis=-1, keepdims=True) + EPS)


_NT = (((1,), (1,)), ((), ()))
_TN = (((0,), (0,)), ((), ()))


def _norm_matmul_body(x_ref, g_ref, w_ref, o_ref, h_ref):
    @pl.when(pl.program_id(1) == 0)
    def _():
        h_ref[...] = (_rms(x_ref[...]) * g_ref[...]).astype(BF16)

    o_ref[...] = jnp.dot(h_ref[...], w_ref[...], preferred_element_type=F32).astype(o_ref.dtype)


def _norm_matmul(x2d, g, w, out_dtype):
    t, d = x2d.shape
    n = w.shape[1]
    tm, tn = _tile(t, TM_PROJ), _tile(n, TN_PROJ)
    return pl.pallas_call(
        _norm_matmul_body,
        grid=(t // tm, n // tn),
        in_specs=[pl.BlockSpec((tm, d), lambda i, j: (i, 0)),
                  pl.BlockSpec((1, d), lambda i, j: (0, 0)),
                  pl.BlockSpec((d, tn), lambda i, j: (0, j))],
        out_specs=pl.BlockSpec((tm, tn), lambda i, j: (i, j)),
        out_shape=jax.ShapeDtypeStruct((t, n), out_dtype),
        scratch_shapes=[pltpu.VMEM((tm, d), BF16)],
        compiler_params=_cparams("parallel", "arbitrary"),
        name="norm_in_proj",
    )(x2d, g.reshape(1, d), w)


def _mla_prep_body(pos_ref, cq_ref, ckv_ref, small_ref, gcq_ref, wuq_ref, gckv_ref, wukv_ref, gq_ref, gk_ref,
                   invf_ref, qT_ref, k_ref, vT_ref):
    tp = cq_ref.shape[1]
    half = MLA_ROPE // 2
    ang = invf_ref[...] * pos_ref[0].astype(F32)
    cos, sin = jnp.cos(ang), jnp.sin(ang)

    def rope(t):
        x1, x2 = t[:half], t[half:]
        return x1 * cos - x2 * sin, x2 * cos + x1 * sin

    cqn = (_rms(cq_ref[0].astype(F32)) * gcq_ref[...]).astype(BF16)
    ckvn = (_rms(ckv_ref[0].astype(F32)) * gckv_ref[...]).astype(BF16)
    qT = lax.dot_general(wuq_ref[...], cqn, _NT, preferred_element_type=F32)
    kvT = lax.dot_general(wukv_ref[...], ckvn, _NT, preferred_element_type=F32)
    kr = small_ref[0].astype(F32).T[:MLA_ROPE]
    kr_ss = jnp.sum(kr * kr, axis=0, keepdims=True)
    gq, gk = gq_ref[...], gk_ref[...]
    zpad = jnp.zeros((HEAD_PAD - MLA_QK, tp), F32)
    vrow = lax.broadcasted_iota(jnp.int32, (V_ROWS - MLA_V, tp), 0)
    vpad = jnp.where(vrow == 0, 1.0, 0.0).astype(F32)
    scale = MLA_QK ** -0.5 * np.log2(np.e)
    for h in range(MLA_HEADS):
        qh = qT[h * MLA_QK:(h + 1) * MLA_QK]
        qn = qh * lax.rsqrt(jnp.mean(qh * qh, axis=0, keepdims=True) + EPS) * (gq * scale)
        r1, r2 = rope(qn[MLA_NOPE:])
        qT_ref[0, h] = jnp.concatenate([qn[:MLA_NOPE], r1, r2, zpad], axis=0).astype(BF16)
        base = h * (MLA_NOPE + MLA_V)
        kn = kvT[base:base + MLA_NOPE]
        r = lax.rsqrt((jnp.sum(kn * kn, axis=0, keepdims=True) + kr_ss) * (1.0 / MLA_QK) + EPS)
        r1, r2 = rope(kr * r * gk[MLA_NOPE:])
        kfm = jnp.concatenate([kn * r * gk[:MLA_NOPE], r1, r2, zpad], axis=0)
        k_ref[0, h] = kfm.T.astype(BF16)
        vT_ref[0, h, 0] = jnp.concatenate([kvT[base + MLA_NOPE:base + MLA_NOPE + MLA_V], vpad], axis=0).astype(BF16)


def _mla_prep(proj, positions, g_cq, w_uq, g_ckv, w_ukv, g_qk_q, g_qk_k):
    b, s, _ = proj.shape
    tp = _tile(s, ATT_T)
    ns = s // tp
    h = MLA_HEADS
    pos = positions.reshape(b * ns, 1, tp)
    inv_freq = (ROPE_BASE ** (-jnp.arange(MLA_ROPE // 2, dtype=F32) / (MLA_ROPE // 2))).reshape(-1, 1)
    const = lambda shape: pl.BlockSpec(shape, lambda bi, si: (0,) * len(shape))
    return pl.pallas_call(
        _mla_prep_body,
        grid=(b, ns),
        in_specs=[pl.BlockSpec((1, 1, tp), lambda bi, si: (bi * ns + si, 0, 0)),
                  pl.BlockSpec((1, tp, MLA_Q_RANK), lambda bi, si: (bi, si, COL_CQ // MLA_Q_RANK)),
                  pl.BlockSpec((1, tp, MLA_KV_RANK), lambda bi, si: (bi, si, COL_CKV // MLA_KV_RANK)),
                  pl.BlockSpec((1, tp, LANES), lambda bi, si: (bi, si, COL_SMALL // LANES)),
                  const((1, MLA_Q_RANK)), const((h * MLA_QK, MLA_Q_RANK)),
                  const((1, MLA_KV_RANK)), const((h * (MLA_NOPE + MLA_V), MLA_KV_RANK)),
                  const((MLA_QK, 1)), const((MLA_QK, 1)), const((MLA_ROPE // 2, 1))],
        out_specs=[pl.BlockSpec((1, h, HEAD_PAD, tp), lambda bi, si: (bi, 0, 0, si)),
                   pl.BlockSpec((1, h, tp, HEAD_PAD), lambda bi, si: (bi, 0, si, 0)),
                   pl.BlockSpec((1, h, 1, V_ROWS, tp), lambda bi, si: (bi, 0, si, 0, 0))],
        out_shape=[jax.ShapeDtypeStruct((b, h, HEAD_PAD, s), BF16),
                   jax.ShapeDtypeStruct((b, h, s, HEAD_PAD), BF16),
                   jax.ShapeDtypeStruct((b, h, ns, V_ROWS, tp), BF16)],
        compiler_params=_cparams("parallel", "parallel"),
        name="mla_prep",
    )(pos, proj, proj, proj, g_cq.reshape(1, -1), w_uq.T.astype(BF16), g_ckv.reshape(1, -1), w_ukv.T.astype(BF16),
      g_qk_q.reshape(-1, 1), g_qk_k.reshape(-1, 1), inv_freq)


def _attn_body(qT_ref, k_ref, vT_ref, o_ref, *, tq, nh):
    i = pl.program_id(2)

    def step(j, carry, masked):
        scores = []
        for h in range(nh):
            kb = k_ref[0, h, pl.ds(pl.multiple_of(j * tq, tq), tq), :]
            s = jnp.dot(kb, qT_ref[0, h], preferred_element_type=F32)
            if masked:
                kpos = lax.broadcasted_iota(jnp.int32, (tq, tq), 0)
                qpos = lax.broadcasted_iota(jnp.int32, (tq, tq), 1)
                s = jnp.where(kpos <= qpos, s, -jnp.inf)
            scores.append(s)
        out = []
        for h in range(nh):
            m, acc = carry[h]
            m_new = jnp.maximum(m, jnp.max(scores[h], axis=0, keepdims=True))
            p = jnp.exp2(scores[h] - m_new).astype(BF16)
            acc = jnp.exp2(m - m_new) * acc + jnp.dot(vT_ref[0, h, j], p, preferred_element_type=F32)
            out.append((m_new, acc))
        return tuple(out)

    carry = tuple((jnp.full((1, tq), -jnp.inf, F32), jnp.zeros((V_ROWS, tq), F32)) for _ in range(nh))
    carry = lax.fori_loop(0, i, functools.partial(step, masked=False), carry)
    carry = step(i, carry, True)
    for h in range(nh):
        acc = carry[h][1]
        o_ref[0, h] = (acc[:MLA_V] / acc[MLA_V:MLA_V + 1]).astype(o_ref.dtype)


def _attention(qT, k, vT):
    b, h, _, s = qT.shape
    tq = vT.shape[-1]
    nk = s // tq
    nh = ATT_NH
    return pl.pallas_call(
        functools.partial(_attn_body, tq=tq, nh=nh),
        grid=(b, h // nh, s // tq),
        in_specs=[pl.BlockSpec((1, nh, HEAD_PAD, tq), lambda bi, hi, qi: (bi, hi, 0, qi)),
                  pl.BlockSpec((1, nh, s, HEAD_PAD), lambda bi, hi, qi: (bi, hi, 0, 0)),
                  pl.BlockSpec((1, nh, nk, V_ROWS, tq), lambda bi, hi, qi: (bi, hi, 0, 0, 0))],
        out_specs=pl.BlockSpec((1, nh, MLA_V, tq), lambda bi, hi, qi: (bi, hi, 0, qi)),
        out_shape=jax.ShapeDtypeStruct((b, h, MLA_V, s), BF16),
        compiler_params=_cparams("parallel", "parallel", "arbitrary"),
        name="mla_attention",
    )(qT, k, vT)


def _gla_body(q_ref, k_ref, v_ref, og_ref, small_ref, wg_ref, bg_ref, gout_ref, o_ref, st_ref, *, blk):
    @pl.when(pl.program_id(1) == 0)
    def _():
        st_ref[...] = jnp.zeros_like(st_ref)

    hd = GLA_HEADS * GLA_DK
    lr = small_ref[0][:, SMALL_LR:SMALL_LR + GLA_GATE_RANK].astype(F32)
    z = jnp.dot(lr, wg_ref[...], precision=HIGHEST, preferred_element_type=F32) + bg_ref[...]
    log_a = jax.nn.log_sigmoid(z) * (1.0 / GLA_TAU)
    row = lax.broadcasted_iota(jnp.int32, (blk, blk), 0)
    col = lax.broadcasted_iota(jnp.int32, (blk, blk), 1)
    bc = jnp.dot((col <= row).astype(F32), log_a, precision=HIGHEST, preferred_element_type=F32)
    q = q_ref[0].astype(F32) * (GLA_DK ** -0.5)
    k = k_ref[0].astype(F32)

    def ref_rows(group, off):
        parts = [jnp.broadcast_to(bc[g0 + off:g0 + off + 1], (group, hd)) for g0 in range(0, blk, group)]
        return parts[0] if len(parts) == 1 else jnp.concatenate(parts, axis=0)

    levels = []
    group = blk
    while group > GLA_BASE:
        half = group // 2
        ref = ref_rows(group, half - 1)
        qs = (q * jnp.exp(jnp.minimum(bc - ref, 0.0))).astype(BF16)
        ks = (k * jnp.exp(jnp.minimum(ref - bc, 0.0))).astype(BF16)
        mask = ((row & -group) == (col & -group)) & ((row & half) != 0) & ((col & half) == 0)
        levels.append((qs, ks, mask))
        group = half
    if GLA_BASE == 1:
        levels.append((q.astype(BF16), k.astype(BF16), row == col))
    else:
        ref = ref_rows(GLA_BASE, GLA_BASE // 2 - 1)
        mask = ((row & -GLA_BASE) == (col & -GLA_BASE)) & (col <= row)
        levels.append(((q * jnp.exp(bc - ref)).astype(BF16), (k * jnp.exp(ref - bc)).astype(BF16), mask))

    qd = (q * jnp.exp(bc)).astype(BF16)
    b_last = bc[blk - 1:blk]
    kd = (k * jnp.exp(b_last - bc)).astype(BF16)
    a_last = jnp.exp(b_last)
    gout = gout_ref[...]
    for h in range(GLA_HEADS):
        ks_, vs_ = slice(h * GLA_DK, (h + 1) * GLA_DK), slice(h * GLA_DV, (h + 1) * GLA_DV)
        attn = jnp.zeros((blk, blk), F32)
        for qs, ksc, mask in levels:
            p = lax.dot_general(qs[:, ks_], ksc[:, ks_], _NT, preferred_element_type=F32)
            attn = jnp.where(mask, p, attn)
        v_h = v_ref[0, :, vs_]
        st = st_ref[h]
        o = jnp.dot(attn.astype(BF16), v_h, preferred_element_type=F32)
        o = o + lax.dot_general(qd[:, ks_], st.astype(BF16), _NT, preferred_element_type=F32)
        st_ref[h] = st * a_last[:, ks_] + lax.dot_general(v_h, kd[:, ks_], _TN, preferred_element_type=F32)
        og = og_ref[0, :, vs_].astype(F32)
        o_ref[0, :, vs_] = (_rms(o) * gout * (og * jax.nn.sigmoid(og))).astype(o_ref.dtype)


def _gla(proj, w_gate_up, b_gate, g_out):
    b, s, _ = proj.shape
    blk = _tile(s, GLA_L)
    hd, hv = GLA_HEADS * GLA_DK, GLA_HEADS * GLA_DV
    const = lambda shape: pl.BlockSpec(shape, lambda bi, si: (0,) * len(shape))
    return pl.pallas_call(
        functools.partial(_gla_body, blk=blk),
        grid=(b, s // blk),
        in_specs=[pl.BlockSpec((1, blk, hd), lambda bi, si: (bi, si, COL_GQ // hd)),
                  pl.BlockSpec((1, blk, hd), lambda bi, si: (bi, si, COL_GK // hd)),
                  pl.BlockSpec((1, blk, hv), lambda bi, si: (bi, si, COL_GV // hv)),
                  pl.BlockSpec((1, blk, hv), lambda bi, si: (bi, si, COL_GOG // hv)),
                  pl.BlockSpec((1, blk, LANES), lambda bi, si: (bi, si, COL_SMALL // LANES)),
                  const((GLA_GATE_RANK, hd)), const((1, hd)), const((1, GLA_DV))],
        out_specs=pl.BlockSpec((1, blk, hv), lambda bi, si: (bi, si, 0)),
        out_shape=jax.ShapeDtypeStruct((b, s, hv), BF16),
        scratch_shapes=[pltpu.VMEM((GLA_HEADS, GLA_DV, GLA_DK), F32)],
        compiler_params=_cparams("parallel", "arbitrary"),
        name="gla",
    )(proj, proj, proj, proj, proj, w_gate_up, b_gate.reshape(1, -1), g_out.reshape(1, -1))


def _pool_body(u_ref, halo_ref, w_ref, scale_ref, o_ref, *, tm):
    si = pl.program_id(1)
    row = lax.broadcasted_iota(jnp.int32, (tm, tm + POOL_HALO), 0)
    col = lax.broadcasted_iota(jnp.int32, (tm, tm + POOL_HALO), 1)
    t1 = (si * tm + lax.broadcasted_iota(jnp.int32, (tm, 1), 0) + 1).astype(F32)
    for g, win in enumerate(POOL_WINDOWS):
        cs = slice(g * POOL_GROUP, (g + 1) * POOL_GROUP)
        cur = u_ref[0, :, cs]
        halo = halo_ref[0, :, cs]
        ext = jnp.concatenate([jnp.where(si > 0, halo, jnp.zeros_like(halo)), cur], axis=0)
        band = ((col <= row + POOL_HALO) & (col > row + POOL_HALO - win)).astype(BF16)
        wsum = jnp.dot(band, ext, preferred_element_type=F32)
        pooled = wsum / jnp.minimum(t1, float(win)) - cur.astype(F32)
        y = jnp.dot(pooled.astype(BF16), w_ref[g], preferred_element_type=F32)
        o_ref[0, :, cs] = (y * scale_ref[:, cs]).astype(o_ref.dtype)


def _pool(proj, w_pool, pool_scale):
    b, s, _ = proj.shape
    tm = _tile(s, TM_POOL)
    hb = tm // POOL_HALO
    return pl.pallas_call(
        functools.partial(_pool_body, tm=tm),
        grid=(b, s // tm),
        in_specs=[pl.BlockSpec((1, tm, POOL_WIDTH), lambda bi, si: (bi, si, COL_POOL // POOL_WIDTH)),
                  pl.BlockSpec((1, POOL_HALO, POOL_WIDTH),
                               lambda bi, si: (bi, jnp.maximum(si * hb - 1, 0), COL_POOL // POOL_WIDTH)),
                  pl.BlockSpec((len(POOL_WINDOWS), POOL_GROUP, POOL_GROUP), lambda bi, si: (0, 0, 0)),
                  pl.BlockSpec((1, POOL_WIDTH), lambda bi, si: (0, 0))],
        out_specs=pl.BlockSpec((1, tm, POOL_WIDTH), lambda bi, si: (bi, si, 0)),
        out_shape=jax.ShapeDtypeStruct((b, s, POOL_WIDTH), BF16),
        compiler_params=_cparams("parallel", "parallel"),
        name="pool",
    )(proj, proj, w_pool.astype(BF16), pool_scale.reshape(1, -1))


def _merge_body(x_ref, yaT_ref, yb_ref, yc_ref, g0_ref, g1_ref, g2_ref, wb_ref, wo_ref, o_ref):
    d_a = lax.dot_general(yaT_ref[0], wb_ref[0], _TN, preferred_element_type=F32)
    d_b = jnp.dot(yb_ref[0], wb_ref[1], preferred_element_type=F32)
    d_c = jnp.dot(yc_ref[0], wb_ref[2], preferred_element_type=F32)
    merged = (jax.nn.sigmoid(g0_ref[0].astype(F32)) * d_a + jax.nn.sigmoid(g1_ref[0].astype(F32)) * d_b
              + jax.nn.sigmoid(g2_ref[0].astype(F32)) * d_c)
    o_ref[0] = x_ref[0] + jnp.dot(merged.astype(BF16), wo_ref[...], preferred_element_type=F32)


def _merge(x, yaT, yb, yc, proj, w_branch, w_out):
    b, s, d = x.shape
    tm = _tile(s, TM_MERGE)
    tok = lambda width, cb: pl.BlockSpec((1, tm, width), lambda bi, si: (bi, si, cb))
    return pl.pallas_call(
        _merge_body,
        grid=(b, s // tm),
        in_specs=[tok(d, 0),
                  pl.BlockSpec((1, BRANCH_WIDTH, tm), lambda bi, si: (bi, 0, si)),
                  tok(BRANCH_WIDTH, 0), tok(BRANCH_WIDTH, 0),
                  tok(d, COL_GATES // d), tok(d, COL_GATES // d + 1), tok(d, COL_GATES // d + 2),
                  pl.BlockSpec((N_BRANCH, BRANCH_WIDTH, d), lambda bi, si: (0, 0, 0)),
                  pl.BlockSpec((d, d), lambda bi, si: (0, 0))],
        out_specs=tok(d, 0),
        out_shape=jax.ShapeDtypeStruct((b, s, d), F32),
        compiler_params=_cparams("parallel", "parallel"),
        name="merge_out_proj",
    )(x, yaT, yb, yc, proj, proj, proj, w_branch.astype(BF16), w_out.astype(BF16))


def _ffn_body(x_ref, g_ref, wg_ref, wu_ref, wd_ref, o_ref, h_ref, acc_ref):
    f = pl.program_id(1)

    @pl.when(f == 0)
    def _():
        h_ref[...] = (_rms(x_ref[...]) * g_ref[...]).astype(BF16)
        acc_ref[...] = jnp.zeros_like(acc_ref)

    h = h_ref[...]
    a = jnp.dot(h, wg_ref[...], preferred_element_type=F32)
    u = jnp.dot(h, wu_ref[...], preferred_element_type=F32)
    act = (a * jax.nn.sigmoid(a) * u).astype(BF16)
    acc_ref[...] += jnp.dot(act, wd_ref[...], preferred_element_type=F32)

    @pl.when(f == pl.num_programs(1) - 1)
    def _():
        o_ref[...] = x_ref[...] + acc_ref[...]


def _ffn(x2d, g, w_gate, w_up, w_down):
    t, d = x2d.shape
    ff = w_gate.shape[1]
    tm, tf = _tile(t, TM_FFN), _tile(ff, TF_FFN)
    return pl.pallas_call(
        _ffn_body,
        grid=(t // tm, ff // tf),
        in_specs=[pl.BlockSpec((tm, d), lambda i, f: (i, 0)),
                  pl.BlockSpec((1, d), lambda i, f: (0, 0)),
                  pl.BlockSpec((d, tf), lambda i, f: (0, f)),
                  pl.BlockSpec((d, tf), lambda i, f: (0, f)),
                  pl.BlockSpec((tf, d), lambda i, f: (f, 0))],
        out_specs=pl.BlockSpec((tm, d), lambda i, f: (i, 0)),
        out_shape=jax.ShapeDtypeStruct((t, d), F32),
        scratch_shapes=[pltpu.VMEM((tm, d), BF16), pltpu.VMEM((tm, d), F32)],
        compiler_params=_cparams("parallel", "arbitrary"),
        name="ffn_dense",
    )(x2d, g.reshape(1, d), w_gate.astype(BF16), w_up.astype(BF16), w_down.astype(BF16))


ROW_TILE = (8, LANES)


def _store_row_tiles(ref, val):
    for c in range(ROW_TILE[0]):
        ref[:, c, :] = val[:, c * LANES:(c + 1) * LANES]


def _load_row_tiles(ref_view):
    return jnp.concatenate([ref_view[:, c, :] for c in range(ROW_TILE[0])], axis=1)


def _route_body(x_ref, g_ref, wr_ref, h_ref, idx_ref, w_ref):
    hf = _rms(x_ref[...]) * g_ref[...]
    _store_row_tiles(h_ref, hf)
    logits = jnp.dot(hf, wr_ref[...], precision=HIGHEST, preferred_element_type=F32)
    lane = lax.broadcasted_iota(jnp.int32, logits.shape, 1)
    m1 = jnp.max(logits, axis=-1, keepdims=True)
    i1 = jnp.min(jnp.where(logits == m1, lane, N_EXPERTS), axis=-1, keepdims=True)
    rest = jnp.where(lane == i1, -jnp.inf, logits)
    m2 = jnp.max(rest, axis=-1, keepdims=True)
    i2 = jnp.min(jnp.where(rest == m2, lane, N_EXPERTS), axis=-1, keepdims=True)
    e2 = jnp.exp(m2 - m1)
    w1 = 1.0 / (1.0 + e2)
    first = lax.broadcasted_iota(jnp.int32, idx_ref.shape, 1) == 0
    idx_ref[...] = jnp.where(first, i1, i2)
    w_ref[...] = jnp.where(first, w1, e2 * w1)


def _moe_route(x2d, g, w_router):
    t, d = x2d.shape
    ne = w_router.shape[1]
    tm = _tile(t, TM_ROUTE)
    return pl.pallas_call(
        _route_body,
        grid=(t // tm,),
        in_specs=[pl.BlockSpec((tm, d), lambda i: (i, 0)),
                  pl.BlockSpec((1, d), lambda i: (0, 0)),
                  pl.BlockSpec((d, ne), lambda i: (0, 0))],
        out_specs=[pl.BlockSpec((tm,) + ROW_TILE, lambda i: (i, 0, 0)),
                   pl.BlockSpec((tm, TOP_K), lambda i: (i, 0)),
                   pl.BlockSpec((tm, TOP_K), lambda i: (i, 0))],
        out_shape=[jax.ShapeDtypeStruct((t,) + ROW_TILE, F32),
                   jax.ShapeDtypeStruct((t, TOP_K), jnp.int32),
                   jax.ShapeDtypeStruct((t, TOP_K), F32)],
        compiler_params=_cparams("parallel"),
        name="moe_route",
    )(x2d, g.reshape(1, d), w_router)


def _moe_plan(idx, tm):
    t = idx.shape[0]
    e = idx.reshape(-1)
    onehot = (e[:, None] == jnp.arange(N_EXPERTS, dtype=jnp.int32)[None, :]).astype(jnp.int32)
    csum = jnp.cumsum(onehot, axis=0)
    rank = jnp.sum((csum - onehot) * onehot, axis=1)
    tiles = (csum[-1] + tm - 1) // tm
    tile_end = jnp.cumsum(tiles)
    row0 = (tile_end - tiles) * tm
    dest = jnp.sum(onehot * row0[None, :], axis=1) + rank
    n_tiles = (TOP_K * t) // tm + N_EXPERTS
    row_token = jnp.zeros((n_tiles * tm,), jnp.int32).at[dest].set(jnp.arange(TOP_K * t, dtype=jnp.int32) // TOP_K)
    tile_expert = jnp.sum(jnp.arange(n_tiles, dtype=jnp.int32)[:, None] >= tile_end[None, :], axis=1)
    tile_expert = jnp.minimum(tile_expert, N_EXPERTS - 1).astype(jnp.int32)
    return row_token.reshape(n_tiles, 1, tm), dest.reshape(t, TOP_K), tile_expert, tile_end[-1:].astype(jnp.int32)


def _gather_rows(idx_ref, src_hbm, buf, sem, n):
    def body(r, carry):
        pltpu.make_async_copy(src_hbm.at[pl.ds(idx_ref[0, 0, r], 1)], buf.at[pl.ds(r, 1)], sem).start()
        return carry

    lax.fori_loop(0, n, body, 0)


def _wait_rows(src_hbm, buf, sem, n):
    pltpu.make_async_copy(src_hbm.at[pl.ds(0, n)], buf, sem).wait()


def _moe_ffn_body(te_ref, nu_ref, cur_ref, nxt_ref, h_hbm, wg_ref, wu_ref, wd_ref, o_ref, xbuf, sem, *, tm):
    i = pl.program_id(0)
    slot = i % 2
    n_used = nu_ref[0]

    @pl.when(jnp.logical_and(i == 0, n_used > 0))
    def _():
        _gather_rows(cur_ref, h_hbm, xbuf.at[0], sem.at[0], tm)

    @pl.when(i + 1 < n_used)
    def _():
        _gather_rows(nxt_ref, h_hbm, xbuf.at[1 - slot], sem.at[1 - slot], tm)

    @pl.when(i < n_used)
    def _():
        _wait_rows(h_hbm, xbuf.at[slot], sem.at[slot], tm)
        xs = _load_row_tiles(xbuf.at[slot]).astype(BF16)
        a = jnp.dot(xs, wg_ref[0], preferred_element_type=F32)
        u = jnp.dot(xs, wu_ref[0], preferred_element_type=F32)
        act = (a * jax.nn.sigmoid(a) * u).astype(BF16)
        _store_row_tiles(o_ref, jnp.dot(act, wd_ref[0], preferred_element_type=F32))

    @pl.when(i >= n_used)
    def _():
        o_ref[...] = jnp.zeros_like(o_ref)


def _moe_ffn(h, row_token, tile_expert, n_used, w_gate, w_up, w_down):
    ne, d, fe = w_gate.shape
    assert h.shape[1:] == ROW_TILE and d == ROW_TILE[0] * ROW_TILE[1]
    n_tiles, _, tm = row_token.shape
    grid_spec = pltpu.PrefetchScalarGridSpec(
        num_scalar_prefetch=2,
        grid=(n_tiles,),
        in_specs=[pl.BlockSpec((1, 1, tm), lambda i, te, nu: (i, 0, 0), memory_space=pltpu.SMEM),
                  pl.BlockSpec((1, 1, tm), lambda i, te, nu: (jnp.minimum(i + 1, n_tiles - 1), 0, 0),
                               memory_space=pltpu.SMEM),
                  pl.BlockSpec(memory_space=pl.ANY),
                  pl.BlockSpec((1, d, fe), lambda i, te, nu: (te[i], 0, 0)),
                  pl.BlockSpec((1, d, fe), lambda i, te, nu: (te[i], 0, 0)),
                  pl.BlockSpec((1, fe, d), lambda i, te, nu: (te[i], 0, 0))],
        out_specs=pl.BlockSpec((tm,) + ROW_TILE, lambda i, te, nu: (i, 0, 0)),
        scratch_shapes=[pltpu.VMEM((2, tm) + ROW_TILE, F32), pltpu.SemaphoreType.DMA((2,))])
    return pl.pallas_call(
        functools.partial(_moe_ffn_body, tm=tm),
        grid_spec=grid_spec,
        out_shape=jax.ShapeDtypeStruct((n_tiles * tm,) + ROW_TILE, F32),
        compiler_params=_cparams("arbitrary"),
        name="moe_ffn",
    )(tile_expert, n_used, row_token, row_token, h, w_gate.astype(BF16), w_up.astype(BF16), w_down.astype(BF16))


def _moe_combine_body(cur_ref, nxt_ref, x_ref, w_ref, y_hbm, o_ref, ybuf, sem, *, tc):
    i = pl.program_id(0)
    slot = i % 2
    n = TOP_K * tc

    @pl.when(i == 0)
    def _():
        _gather_rows(cur_ref, y_hbm, ybuf.at[0], sem.at[0], n)

    @pl.when(i + 1 < pl.num_programs(0))
    def _():
        _gather_rows(nxt_ref, y_hbm, ybuf.at[1 - slot], sem.at[1 - slot], n)

    _wait_rows(y_hbm, ybuf.at[slot], sem.at[slot], n)
    w = w_ref[...]
    y0 = _load_row_tiles(ybuf.at[slot, pl.ds(0, tc)])
    y1 = _load_row_tiles(ybuf.at[slot, pl.ds(tc, tc)])
    o_ref[...] = x_ref[...] + w[:, 0:1] * y0 + w[:, 1:2] * y1


def _moe_combine(x2d, ys, dest, w):
    t, d = x2d.shape
    tc = _tile(t, TM_COMBINE)
    nt = t // tc
    dest_tiles = dest.reshape(nt, tc, TOP_K).transpose(0, 2, 1).reshape(nt, 1, TOP_K * tc)
    return pl.pallas_call(
        functools.partial(_moe_combine_body, tc=tc),
        grid=(nt,),
        in_specs=[pl.BlockSpec((1, 1, TOP_K * tc), lambda i: (i, 0, 0), memory_space=pltpu.SMEM),
                  pl.BlockSpec((1, 1, TOP_K * tc), lambda i: (jnp.minimum(i + 1, nt - 1), 0, 0),
                               memory_space=pltpu.SMEM),
                  pl.BlockSpec((tc, d), lambda i: (i, 0)),
                  pl.BlockSpec((tc, TOP_K), lambda i: (i, 0)),
                  pl.BlockSpec(memory_space=pl.ANY)],
        out_specs=pl.BlockSpec((tc, d), lambda i: (i, 0)),
        out_shape=jax.ShapeDtypeStruct((t, d), F32),
        scratch_shapes=[pltpu.VMEM((2, TOP_K * tc) + ROW_TILE, F32), pltpu.SemaphoreType.DMA((2,))],
        compiler_params=_cparams("arbitrary"),
        name="moe_combine",
    )(dest_tiles, dest_tiles, x2d, w, ys)


def _moe(x2d, g, w_router, w_gate, w_up, w_down):
    h, idx, w = _moe_route(x2d, g, w_router)
    row_token, dest, tile_expert, n_used = _moe_plan(idx, _tile(TOP_K * x2d.shape[0], TM_MOE))
    ys = _moe_ffn(h, row_token, tile_expert, n_used, w_gate, w_up, w_down)
    return _moe_combine(x2d, ys, dest, w)


def _permute_w_in(w):
    off = np.concatenate([[0], np.cumsum(IN_SPLITS)]).tolist()
    piece = lambda i: w[:, off[i]:off[i + 1]]
    order = [9, 0, 1, 3, 4, 5, 7, 8, 2, 6]
    pad = jnp.zeros((w.shape[0], D_PROJ - off[-1]), w.dtype)
    return jnp.concatenate([piece(i) for i in order] + [pad], axis=1).astype(BF16)


def kernel(x, positions, g_mix, w_in, g_cq, w_uq, g_ckv, w_ukv, g_qk_q, g_qk_k, w_gla_gate_up, b_gla_gate, g_gla_out, w_pool, pool_scale, w_branch, w_out, g_ffn, w_ffn_gate, w_ffn_up, w_ffn_down, w_router, w_exp_gate, w_exp_up, w_exp_down):
    b, s, d = x.shape
    depth = g_mix.shape[0]
    for layer in range(depth):
        proj = _norm_matmul(x.reshape(b * s, d), g_mix[layer], _permute_w_in(w_in[layer]), BF16).reshape(b, s, D_PROJ)
        qT, k, vT = _mla_prep(proj, positions, g_cq[layer], w_uq[layer], g_ckv[layer], w_ukv[layer],
                              g_qk_q[layer], g_qk_k[layer])
        yaT = _attention(qT, k, vT).reshape(b, MLA_HEADS * MLA_V, s)
        yb = _gla(proj, w_gla_gate_up[layer], b_gla_gate[layer], g_gla_out[layer])
        yc = _pool(proj, w_pool[layer], pool_scale[layer])
        x = _merge(x, yaT, yb, yc, proj, w_branch[layer], w_out[layer])
        i = layer // 2
        if layer % 2 == 0:
            x2d = _ffn(x.reshape(b * s, d), g_ffn[layer], w_ffn_gate[i], w_ffn_up[i], w_ffn_down[i])
        else:
            x2d = _moe(x.reshape(b * s, d), g_ffn[layer], w_router[i], w_exp_gate[i], w_exp_up[i], w_exp_down[i])
        x = x2d.reshape(b, s, d)
    return x
```

```python
import functools

import jax
import jax.numpy as jnp
import numpy as np
from jax import lax
from jax.experimental import pallas as pl
from jax.experimental.pallas import tpu as pltpu

F32 = jnp.float32
BF16 = jnp.bfloat16
HIGHEST = lax.Precision.HIGHEST

D_MODEL = 1024
MLA_HEADS = 8
MLA_NOPE = 64
MLA_ROPE = 32
MLA_QK = MLA_NOPE + MLA_ROPE
MLA_V = 64
MLA_Q_RANK = D_MODEL // 4
MLA_KV_RANK = D_MODEL // 4
ROPE_BASE = 10000.0
GLA_HEADS = 4
GLA_DK = 64
GLA_DV = 128
GLA_GATE_RANK = 16
GLA_TAU = 16.0
POOL_WINDOWS = (2, 4, 8, 16)
POOL_GROUP = 128
POOL_WIDTH = 4 * POOL_GROUP
N_BRANCH = 3
BRANCH_WIDTH = 512
D_FF = 2816
N_EXPERTS = 8
TOP_K = 2
D_EXPERT = 1408
EPS = 1e-6

IN_SPLITS = (MLA_Q_RANK, MLA_KV_RANK, MLA_ROPE,
             GLA_HEADS * GLA_DK, GLA_HEADS * GLA_DK, GLA_HEADS * GLA_DV, GLA_GATE_RANK, GLA_HEADS * GLA_DV,
             POOL_WIDTH, N_BRANCH * D_MODEL)

LANES = 128
HEAD_PAD = LANES

COL_GATES = 0
COL_CQ = 3072
COL_CKV = 3328
COL_GQ = 3584
COL_GK = 3840
COL_GV = 4096
COL_GOG = 4608
COL_POOL = 5120
COL_SMALL = 5632
D_PROJ = 5760
SMALL_LR = MLA_ROPE

TM_PROJ = 1024
TN_PROJ = 1920
ATT_T = 512
ATT_NH = 4
V_ROWS = 80
GLA_L = 128
GLA_BASE = 1
TM_POOL = 256
POOL_HALO = 16
TM_MERGE = 512
TM_FFN = 1024
TF_FFN = 256
TM_ROUTE = 1024
TM_MOE = 512
TM_COMBINE = 512
VMEM_LIMIT = 56 * 1024 * 1024


def _cparams(*sem):
    return pltpu.CompilerParams(dimension_semantics=sem, vmem_limit_bytes=VMEM_LIMIT)


def _tile(n, pref):
    t = min(n, pref)
    assert n % t == 0, (n, pref)
    return t


def _rms(x):
    return x * lax.rsqrt(jnp.mean(x * x, axis=-1, keepdims=True) + EPS)


_NT = (((1,), (1,)), ((), ()))
_TN = (((0,), (0,)), ((), ()))


def _norm_matmul_body(x_ref, g_ref, w_ref, o_ref, h_ref):
    @pl.when(pl.program_id(1) == 0)
    def _():
        h_ref[...] = (_rms(x_ref[...]) * g_ref[...]).astype(BF16)

    o_ref[...] = jnp.dot(h_ref[...], w_ref[...], preferred_element_type=F32).astype(o_ref.dtype)


def _norm_matmul(x2d, g, w, out_dtype):
    t, d = x2d.shape
    n = w.shape[1]
    tm, tn = _tile(t, TM_PROJ), _tile(n, TN_PROJ)
    return pl.pallas_call(
        _norm_matmul_body,
        grid=(t // tm, n // tn),
        in_specs=[pl.BlockSpec((tm, d), lambda i, j: (i, 0)),
                  pl.BlockSpec((1, d), lambda i, j: (0, 0)),
                  pl.BlockSpec((d, tn), lambda i, j: (0, j))],
        out_specs=pl.BlockSpec((tm, tn), lambda i, j: (i, j)),
        out_shape=jax.ShapeDtypeStruct((t, n), out_dtype),
        scratch_shapes=[pltpu.VMEM((tm, d), BF16)],
        compiler_params=_cparams("parallel", "arbitrary"),
        name="norm_in_proj",
    )(x2d, g.reshape(1, d), w)


def _mla_prep_body(pos_ref, cq_ref, ckv_ref, small_ref, gcq_ref, wuq_ref, gckv_ref, wukv_ref, gq_ref, gk_ref,
                   invf_ref, qT_ref, k_ref, vT_ref):
    tp = cq_ref.shape[1]
    half = MLA_ROPE // 2
    ang = invf_ref[...] * pos_ref[0].astype(F32)
    cos, sin = jnp.cos(ang), jnp.sin(ang)

    def rope(t):
        x1, x2 = t[:half], t[half:]
        return x1 * cos - x2 * sin, x2 * cos + x1 * sin

    cqn = (_rms(cq_ref[0].astype(F32)) * gcq_ref[...]).astype(BF16)
    ckvn = (_rms(ckv_ref[0].astype(F32)) * gckv_ref[...]).astype(BF16)
    qT = lax.dot_general(wuq_ref[...], cqn, _NT, preferred_element_type=F32)
    kvT = lax.dot_general(wukv_ref[...], ckvn, _NT, preferred_element_type=F32)
    kr = small_ref[0].astype(F32).T[:MLA_ROPE]
    kr_ss = jnp.sum(kr * kr, axis=0, keepdims=True)
    gq, gk = gq_ref[...], gk_ref[...]
    zpad = jnp.zeros((HEAD_PAD - MLA_QK, tp), F32)
    vrow = lax.broadcasted_iota(jnp.int32, (V_ROWS - MLA_V, tp), 0)
    vpad = jnp.where(vrow == 0, 1.0, 0.0).astype(F32)
    scale = MLA_QK ** -0.5 * np.log2(np.e)
    for h in range(MLA_HEADS):
        qh = qT[h * MLA_QK:(h + 1) * MLA_QK]
        qn = qh * lax.rsqrt(jnp.mean(qh * qh, axis=0, keepdims=True) + EPS) * (gq * scale)
        r1, r2 = rope(qn[MLA_NOPE:])
        qT_ref[0, h] = jnp.concatenate([qn[:MLA_NOPE], r1, r2, zpad], axis=0).astype(BF16)
        base = h * (MLA_NOPE + MLA_V)
        kn = kvT[base:base + MLA_NOPE]
        r = lax.rsqrt((jnp.sum(kn * kn, axis=0, keepdims=True) + kr_ss) * (1.0 / MLA_QK) + EPS)
        r1, r2 = rope(kr * r * gk[MLA_NOPE:])
        kfm = jnp.concatenate([kn * r * gk[:MLA_NOPE], r1, r2, zpad], axis=0)
        k_ref[0, h] = kfm.T.astype(BF16)
        vT_ref[0, h, 0] = jnp.concatenate([kvT[base + MLA_NOPE:base + MLA_NOPE + MLA_V], vpad], axis=0).astype(BF16)


def _mla_prep(proj, positions, g_cq, w_uq, g_ckv, w_ukv, g_qk_q, g_qk_k):
    b, s, _ = proj.shape
    tp = _tile(s, ATT_T)
    ns = s // tp
    h = MLA_HEADS
    pos = positions.reshape(b * ns, 1, tp)
    inv_freq = (ROPE_BASE ** (-jnp.arange(MLA_ROPE // 2, dtype=F32) / (MLA_ROPE // 2))).reshape(-1, 1)
    const = lambda shape: pl.BlockSpec(shape, lambda bi, si: (0,) * len(shape))
    return pl.pallas_call(
        _mla_prep_body,
        grid=(b, ns),
        in_specs=[pl.BlockSpec((1, 1, tp), lambda bi, si: (bi * ns + si, 0, 0)),
                  pl.BlockSpec((1, tp, MLA_Q_RANK), lambda bi, si: (bi, si, COL_CQ // MLA_Q_RANK)),
                  pl.BlockSpec((1, tp, MLA_KV_RANK), lambda bi, si: (bi, si, COL_CKV // MLA_KV_RANK)),
                  pl.BlockSpec((1, tp, LANES), lambda bi, si: (bi, si, COL_SMALL // LANES)),
                  const((1, MLA_Q_RANK)), const((h * MLA_QK, MLA_Q_RANK)),
                  const((1, MLA_KV_RANK)), const((h * (MLA_NOPE + MLA_V), MLA_KV_RANK)),
                  const((MLA_QK, 1)), const((MLA_QK, 1)), const((MLA_ROPE // 2, 1))],
        out_specs=[pl.BlockSpec((1, h, HEAD_PAD, tp), lambda bi, si: (bi, 0, 0, si)),
                   pl.BlockSpec((1, h, tp, HEAD_PAD), lambda bi, si: (bi, 0, si, 0)),
                   pl.BlockSpec((1, h, 1, V_ROWS, tp), lambda bi, si: (bi, 0, si, 0, 0))],
        out_shape=[jax.ShapeDtypeStruct((b, h, HEAD_PAD, s), BF16),
                   jax.ShapeDtypeStruct((b, h, s, HEAD_PAD), BF16),
                   jax.ShapeDtypeStruct((b, h, ns, V_ROWS, tp), BF16)],
        compiler_params=_cparams("parallel", "parallel"),
        name="mla_prep",
    )(pos, proj, proj, proj, g_cq.reshape(1, -1), w_uq.T.astype(BF16), g_ckv.reshape(1, -1), w_ukv.T.astype(BF16),
      g_qk_q.reshape(-1, 1), g_qk_k.reshape(-1, 1), inv_freq)


def _attn_body(qT_ref, k_ref, vT_ref, o_ref, *, tq, nh):
    i = pl.program_id(2)

    def step(j, carry, masked):
        scores = []
        for h in range(nh):
            kb = k_ref[0, h, pl.ds(pl.multiple_of(j * tq, tq), tq), :]
            s = jnp.dot(kb, qT_ref[0, h], preferred_element_type=F32)
            if masked:
                kpos = lax.broadcasted_iota(jnp.int32, (tq, tq), 0)
                qpos = lax.broadcasted_iota(jnp.int32, (tq, tq), 1)
                s = jnp.where(kpos <= qpos, s, -jnp.inf)
            scores.append(s)
        out = []
        for h in range(nh):
            m, acc = carry[h]
            m_new = jnp.maximum(m, jnp.max(scores[h], axis=0, keepdims=True))
            p = jnp.exp2(scores[h] - m_new).astype(BF16)
            acc = jnp.exp2(m - m_new) * acc + jnp.dot(vT_ref[0, h, j], p, preferred_element_type=F32)
            out.append((m_new, acc))
        return tuple(out)

    carry = tuple((jnp.full((1, tq), -jnp.inf, F32), jnp.zeros((V_ROWS, tq), F32)) for _ in range(nh))
    carry = lax.fori_loop(0, i, functools.partial(step, masked=False), carry)
    carry = step(i, carry, True)
    for h in range(nh):
        acc = carry[h][1]
        o_ref[0, h] = (acc[:MLA_V] / acc[MLA_V:MLA_V + 1]).astype(o_ref.dtype)


def _attention(qT, k, vT):
    b, h, _, s = qT.shape
    tq = vT.shape[-1]
    nk = s // tq
    nh = ATT_NH
    return pl.pallas_call(
        functools.partial(_attn_body, tq=tq, nh=nh),
        grid=(b, h // nh, s // tq),
        in_specs=[pl.BlockSpec((1, nh, HEAD_PAD, tq), lambda bi, hi, qi: (bi, hi, 0, qi)),
                  pl.BlockSpec((1, nh, s, HEAD_PAD), lambda bi, hi, qi: (bi, hi, 0, 0)),
                  pl.BlockSpec((1, nh, nk, V_ROWS, tq), lambda bi, hi, qi: (bi, hi, 0, 0, 0))],
        out_specs=pl.BlockSpec((1, nh, MLA_V, tq), lambda bi, hi, qi: (bi, hi, 0, qi)),
        out_shape=jax.ShapeDtypeStruct((b, h, MLA_V, s), BF16),
        compiler_params=_cparams("parallel", "parallel", "arbitrary"),
        name="mla_attention",
    )(qT, k, vT)


def _gla_body(q_ref, k_ref, v_ref, og_ref, small_ref, wg_ref, bg_ref, gout_ref, o_ref, st_ref, *, blk):
    @pl.when(pl.program_id(1) == 0)
    def _():
        st_ref[...] = jnp.zeros_like(st_ref)

    hd = GLA_HEADS * GLA_DK
    lr = small_ref[0][:, SMALL_LR:SMALL_LR + GLA_GATE_RANK].astype(F32)
    z = jnp.dot(lr, wg_ref[...], precision=HIGHEST, preferred_element_type=F32) + bg_ref[...]
    log_a = jax.nn.log_sigmoid(z) * (1.0 / GLA_TAU)
    row = lax.broadcasted_iota(jnp.int32, (blk, blk), 0)
    col = lax.broadcasted_iota(jnp.int32, (blk, blk), 1)
    bc = jnp.dot((col <= row).astype(F32), log_a, precision=HIGHEST, preferred_element_type=F32)
    q = q_ref[0].astype(F32) * (GLA_DK ** -0.5)
    k = k_ref[0].astype(F32)

    def ref_rows(group, off):
        parts = [jnp.broadcast_to(bc[g0 + off:g0 + off + 1], (group, hd)) for g0 in range(0, blk, group)]
        return parts[0] if len(parts) == 1 else jnp.concatenate(parts, axis=0)

    levels = []
    group = blk
    while group > GLA_BASE:
        half = group // 2
        ref = ref_rows(group, half - 1)
        qs = (q * jnp.exp(jnp.minimum(bc - ref, 0.0))).astype(BF16)
        ks = (k * jnp.exp(jnp.minimum(ref - bc, 0.0))).astype(BF16)
        mask = ((row & -group) == (col & -group)) & ((row & half) != 0) & ((col & half) == 0)
        levels.append((qs, ks, mask))
        group = half
    if GLA_BASE == 1:
        levels.append((q.astype(BF16), k.astype(BF16), row == col))
    else:
        ref = ref_rows(GLA_BASE, GLA_BASE // 2 - 1)
        mask = ((row & -GLA_BASE) == (col & -GLA_BASE)) & (col <= row)
        levels.append(((q * jnp.exp(bc - ref)).astype(BF16), (k * jnp.exp(ref - bc)).astype(BF16), mask))

    qd = (q * jnp.exp(bc)).astype(BF16)
    b_last = bc[blk - 1:blk]
    kd = (k * jnp.exp(b_last - bc)).astype(BF16)
    a_last = jnp.exp(b_last)
    gout = gout_ref[...]
    for h in range(GLA_HEADS):
        ks_, vs_ = slice(h * GLA_DK, (h + 1) * GLA_DK), slice(h * GLA_DV, (h + 1) * GLA_DV)
        attn = jnp.zeros((blk, blk), F32)
        for qs, ksc, mask in levels:
            p = lax.dot_general(qs[:, ks_], ksc[:, ks_], _NT, preferred_element_type=F32)
            attn = jnp.where(mask, p, attn)
        v_h = v_ref[0, :, vs_]
        st = st_ref[h]
        o = jnp.dot(attn.astype(BF16), v_h, preferred_element_type=F32)
        o = o + lax.dot_general(qd[:, ks_], st.astype(BF16), _NT, preferred_element_type=F32)
        st_ref[h] = st * a_last[:, ks_] + lax.dot_general(v_h, kd[:, ks_], _TN, preferred_element_type=F32)
        og = og_ref[0, :, vs_].astype(F32)
        o_ref[0, :, vs_] = (_rms(o) * gout * (og * jax.nn.sigmoid(og))).astype(o_ref.dtype)


def _gla(proj, w_gate_up, b_gate, g_out):
    b, s, _ = proj.shape
    blk = _tile(s, GLA_L)
    hd, hv = GLA_HEADS * GLA_DK, GLA_HEADS * GLA_DV
    const = lambda shape: pl.BlockSpec(shape, lambda bi, si: (0,) * len(shape))
    return pl.pallas_call(
        functools.partial(_gla_body, blk=blk),
        grid=(b, s // blk),
        in_specs=[pl.BlockSpec((1, blk, hd), lambda bi, si: (bi, si, COL_GQ // hd)),
                  pl.BlockSpec((1, blk, hd), lambda bi, si: (bi, si, COL_GK // hd)),
                  pl.BlockSpec((1, blk, hv), lambda bi, si: (bi, si, COL_GV // hv)),
                  pl.BlockSpec((1, blk, hv), lambda bi, si: (bi, si, COL_GOG // hv)),
                  pl.BlockSpec((1, blk, LANES), lambda bi, si: (bi, si, COL_SMALL // LANES)),
                  const((GLA_GATE_RANK, hd)), const((1, hd)), const((1, GLA_DV))],
        out_specs=pl.BlockSpec((1, blk, hv), lambda bi, si: (bi, si, 0)),
        out_shape=jax.ShapeDtypeStruct((b, s, hv), BF16),
        scratch_shapes=[pltpu.VMEM((GLA_HEADS, GLA_DV, GLA_DK), F32)],
        compiler_params=_cparams("parallel", "arbitrary"),
        name="gla",
    )(proj, proj, proj, proj, proj, w_gate_up, b_gate.reshape(1, -1), g_out.reshape(1, -1))


def _pool_body(u_ref, halo_ref, w_ref, scale_ref, o_ref, *, tm):
    si = pl.program_id(1)
    row = lax.broadcasted_iota(jnp.int32, (tm, tm + POOL_HALO), 0)
    col = lax.broadcasted_iota(jnp.int32, (tm, tm + POOL_HALO), 1)
    t1 = (si * tm + lax.broadcasted_iota(jnp.int32, (tm, 1), 0) + 1).astype(F32)
    for g, win in enumerate(POOL_WINDOWS):
        cs = slice(g * POOL_GROUP, (g + 1) * POOL_GROUP)
        cur = u_ref[0, :, cs]
        halo = halo_ref[0, :, cs]
        ext = jnp.concatenate([jnp.where(si > 0, halo, jnp.zeros_like(halo)), cur], axis=0)
        band = ((col <= row + POOL_HALO) & (col > row + POOL_HALO - win)).astype(BF16)
        wsum = jnp.dot(band, ext, preferred_element_type=F32)
        pooled = wsum / jnp.minimum(t1, float(win)) - cur.astype(F32)
        y = jnp.dot(pooled.astype(BF16), w_ref[g], preferred_element_type=F32)
        o_ref[0, :, cs] = (y * scale_ref[:, cs]).astype(o_ref.dtype)


def _pool(proj, w_pool, pool_scale):
    b, s, _ = proj.shape
    tm = _tile(s, TM_POOL)
    hb = tm // POOL_HALO
    return pl.pallas_call(
        functools.partial(_pool_body, tm=tm),
        grid=(b, s // tm),
        in_specs=[pl.BlockSpec((1, tm, POOL_WIDTH), lambda bi, si: (bi, si, COL_POOL // POOL_WIDTH)),
                  pl.BlockSpec((1, POOL_HALO, POOL_WIDTH),
                               lambda bi, si: (bi, jnp.maximum(si * hb - 1, 0), COL_POOL // POOL_WIDTH)),
                  pl.BlockSpec((len(POOL_WINDOWS), POOL_GROUP, POOL_GROUP), lambda bi, si: (0, 0, 0)),
                  pl.BlockSpec((1, POOL_WIDTH), lambda bi, si: (0, 0))],
        out_specs=pl.BlockSpec((1, tm, POOL_WIDTH), lambda bi, si: (bi, si, 0)),
        out_shape=jax.ShapeDtypeStruct((b, s, POOL_WIDTH), BF16),
        compiler_params=_cparams("parallel", "parallel"),
        name="pool",
    )(proj, proj, w_pool.astype(BF16), pool_scale.reshape(1, -1))


def _merge_body(x_ref, yaT_ref, yb_ref, yc_ref, g0_ref, g1_ref, g2_ref, wb_ref, wo_ref, o_ref):
    d_a = lax.dot_general(yaT_ref[0], wb_ref[0], _TN, preferred_element_type=F32)
    d_b = jnp.dot(yb_ref[0], wb_ref[1], preferred_element_type=F32)
    d_c = jnp.dot(yc_ref[0], wb_ref[2], preferred_element_type=F32)
    merged = (jax.nn.sigmoid(g0_ref[0].astype(F32)) * d_a + jax.nn.sigmoid(g1_ref[0].astype(F32)) * d_b
              + jax.nn.sigmoid(g2_ref[0].astype(F32)) * d_c)
    o_ref[0] = x_ref[0] + jnp.dot(merged.astype(BF16), wo_ref[...], preferred_element_type=F32)


def _merge(x, yaT, yb, yc, proj, w_branch, w_out):
    b, s, d = x.shape
    tm = _tile(s, TM_MERGE)
    tok = lambda width, cb: pl.BlockSpec((1, tm, width), lambda bi, si: (bi, si, cb))
    return pl.pallas_call(
        _merge_body,
        grid=(b, s // tm),
        in_specs=[tok(d, 0),
                  pl.BlockSpec((1, BRANCH_WIDTH, tm), lambda bi, si: (bi, 0, si)),
                  tok(BRANCH_WIDTH, 0), tok(BRANCH_WIDTH, 0),
                  tok(d, COL_GATES // d), tok(d, COL_GATES // d + 1), tok(d, COL_GATES // d + 2),
                  pl.BlockSpec((N_BRANCH, BRANCH_WIDTH, d), lambda bi, si: (0, 0, 0)),
                  pl.BlockSpec((d, d), lambda bi, si: (0, 0))],
        out_specs=tok(d, 0),
        out_shape=jax.ShapeDtypeStruct((b, s, d), F32),
        compiler_params=_cparams("parallel", "parallel"),
        name="merge_out_proj",
    )(x, yaT, yb, yc, proj, proj, proj, w_branch.astype(BF16), w_out.astype(BF16))


def _ffn_body(x_ref, g_ref, wg_ref, wu_ref, wd_ref, o_ref, h_ref, acc_ref):
    f = pl.program_id(1)

    @pl.when(f == 0)
    def _():
        h_ref[...] = (_rms(x_ref[...]) * g_ref[...]).astype(BF16)
        acc_ref[...] = jnp.zeros_like(acc_ref)

    h = h_ref[...]
    a = jnp.dot(h, wg_ref[...], preferred_element_type=F32)
    u = jnp.dot(h, wu_ref[...], preferred_element_type=F32)
    act = (a * jax.nn.sigmoid(a) * u).astype(BF16)
    acc_ref[...] += jnp.dot(act, wd_ref[...], preferred_element_type=F32)

    @pl.when(f == pl.num_programs(1) - 1)
    def _():
        o_ref[...] = x_ref[...] + acc_ref[...]


def _ffn(x2d, g, w_gate, w_up, w_down):
    t, d = x2d.shape
    ff = w_gate.shape[1]
    tm, tf = _tile(t, TM_FFN), _tile(ff, TF_FFN)
    return pl.pallas_call(
        _ffn_body,
        grid=(t // tm, ff // tf),
        in_specs=[pl.BlockSpec((tm, d), lambda i, f: (i, 0)),
                  pl.BlockSpec((1, d), lambda i, f: (0, 0)),
                  pl.BlockSpec((d, tf), lambda i, f: (0, f)),
                  pl.BlockSpec((d, tf), lambda i, f: (0, f)),
                  pl.BlockSpec((tf, d), lambda i, f: (f, 0))],
        out_specs=pl.BlockSpec((tm, d), lambda i, f: (i, 0)),
        out_shape=jax.ShapeDtypeStruct((t, d), F32),
        scratch_shapes=[pltpu.VMEM((tm, d), BF16), pltpu.VMEM((tm, d), F32)],
        compiler_params=_cparams("parallel", "arbitrary"),
        name="ffn_dense",
    )(x2d, g.reshape(1, d), w_gate.astype(BF16), w_up.astype(BF16), w_down.astype(BF16))


ROW_TILE = (8, LANES)


def _store_row_tiles(ref, val):
    for c in range(ROW_TILE[0]):
        ref[:, c, :] = val[:, c * LANES:(c + 1) * LANES]


def _load_row_tiles(ref_view):
    return jnp.concatenate([ref_view[:, c, :] for c in range(ROW_TILE[0])], axis=1)


def _route_body(x_ref, g_ref, wr_ref, h_ref, idx_ref, w_ref):
    hf = _rms(x_ref[...]) * g_ref[...]
    _store_row_tiles(h_ref, hf)
    logits = jnp.dot(hf, wr_ref[...], precision=HIGHEST, preferred_element_type=F32)
    lane = lax.broadcasted_iota(jnp.int32, logits.shape, 1)
    m1 = jnp.max(logits, axis=-1, keepdims=True)
    i1 = jnp.min(jnp.where(logits == m1, lane, N_EXPERTS), axis=-1, keepdims=True)
    rest = jnp.where(lane == i1, -jnp.inf, logits)
    m2 = jnp.max(rest, axis=-1, keepdims=True)
    i2 = jnp.min(jnp.where(rest == m2, lane, N_EXPERTS), axis=-1, keepdims=True)
    e2 = jnp.exp(m2 - m1)
    w1 = 1.0 / (1.0 + e2)
    first = lax.broadcasted_iota(jnp.int32, idx_ref.shape, 1) == 0
    idx_ref[...] = jnp.where(first, i1, i2)
    w_ref[...] = jnp.where(first, w1, e2 * w1)


def _moe_route(x2d, g, w_router):
    t, d = x2d.shape
    ne = w_router.shape[1]
    tm = _tile(t, TM_ROUTE)
    return pl.pallas_call(
        _route_body,
        grid=(t // tm,),
        in_specs=[pl.BlockSpec((tm, d), lambda i: (i, 0)),
                  pl.BlockSpec((1, d), lambda i: (0, 0)),
                  pl.BlockSpec((d, ne), lambda i: (0, 0))],
        out_specs=[pl.BlockSpec((tm,) + ROW_TILE, lambda i: (i, 0, 0)),
                   pl.BlockSpec((tm, TOP_K), lambda i: (i, 0)),
                   pl.BlockSpec((tm, TOP_K), lambda i: (i, 0))],
        out_shape=[jax.ShapeDtypeStruct((t,) + ROW_TILE, F32),
                   jax.ShapeDtypeStruct((t, TOP_K), jnp.int32),
                   jax.ShapeDtypeStruct((t, TOP_K), F32)],
        compiler_params=_cparams("parallel"),
        name="moe_route",
    )(x2d, g.reshape(1, d), w_router)


def _moe_plan(idx, tm):
    t = idx.shape[0]
    e = idx.reshape(-1)
    onehot = (e[:, None] == jnp.arange(N_EXPERTS, dtype=jnp.int32)[None, :]).astype(jnp.int32)
    csum = jnp.cumsum(onehot, axis=0)
    rank = jnp.sum((csum - onehot) * onehot, axis=1)
    tiles = (csum[-1] + tm - 1) // tm
    tile_end = jnp.cumsum(tiles)
    row0 = (tile_end - tiles) * tm
    dest = jnp.sum(onehot * row0[None, :], axis=1) + rank
    n_tiles = (TOP_K * t) // tm + N_EXPERTS
    row_token = jnp.zeros((n_tiles * tm,), jnp.int32).at[dest].set(jnp.arange(TOP_K * t, dtype=jnp.int32) // TOP_K)
    tile_expert = jnp.sum(jnp.arange(n_tiles, dtype=jnp.int32)[:, None] >= tile_end[None, :], axis=1)
    tile_expert = jnp.minimum(tile_expert, N_EXPERTS - 1).astype(jnp.int32)
    return row_token.reshape(n_tiles, 1, tm), dest.reshape(t, TOP_K), tile_expert, tile_end[-1:].astype(jnp.int32)


def _gather_rows(idx_ref, src_hbm, buf, sem, lo, n):
    def body(p, carry):
        for q in range(2):
            r = lo + 2 * p + q
            pltpu.make_async_copy(src_hbm.at[pl.ds(idx_ref[0, 0, r], 1)], buf.at[pl.ds(r, 1)], sem).start(priority=q)
        return carry

    lax.fori_loop(0, n // 2, body, 0)


def _wait_rows(src_hbm, buf, sem, n):
    pltpu.make_async_copy(src_hbm.at[pl.ds(0, n)], buf, sem).wait()


def _moe_ffn_body(te_ref, nu_ref, cur_ref, nxt_ref, h_hbm, wg_ref, wu_ref, wd_ref, o_ref, xbuf, sem, *, tm):
    i = pl.program_id(0)
    slot = i % 2
    n_used = nu_ref[0]

    @pl.when(jnp.logical_and(i == 0, n_used > 0))
    def _():
        _gather_rows(cur_ref, h_hbm, xbuf.at[0], sem.at[0], 0, tm)

    def prefetch(part):
        @pl.when(i + 1 < n_used)
        def _():
            _gather_rows(nxt_ref, h_hbm, xbuf.at[1 - slot], sem.at[1 - slot], part * (tm // 4), tm // 4)

    @pl.when(i < n_used)
    def _():
        _wait_rows(h_hbm, xbuf.at[slot], sem.at[slot], tm)
        xs = _load_row_tiles(xbuf.at[slot]).astype(BF16)
        prefetch(0)
        a = jnp.dot(xs, wg_ref[0], preferred_element_type=F32)
        prefetch(1)
        u = jnp.dot(xs, wu_ref[0], preferred_element_type=F32)
        prefetch(2)
        act = (a * jax.nn.sigmoid(a) * u).astype(BF16)
        y = jnp.dot(act, wd_ref[0], preferred_element_type=F32)
        prefetch(3)
        _store_row_tiles(o_ref, y)

    @pl.when(i >= n_used)
    def _():
        o_ref[...] = jnp.zeros_like(o_ref)


def _moe_ffn(h, row_token, tile_expert, n_used, w_gate, w_up, w_down):
    ne, d, fe = w_gate.shape
    assert h.shape[1:] == ROW_TILE and d == ROW_TILE[0] * ROW_TILE[1]
    n_tiles, _, tm = row_token.shape
    grid_spec = pltpu.PrefetchScalarGridSpec(
        num_scalar_prefetch=2,
        grid=(n_tiles,),
        in_specs=[pl.BlockSpec((1, 1, tm), lambda i, te, nu: (i, 0, 0), memory_space=pltpu.SMEM),
                  pl.BlockSpec((1, 1, tm), lambda i, te, nu: (jnp.minimum(i + 1, n_tiles - 1), 0, 0),
                               memory_space=pltpu.SMEM),
                  pl.BlockSpec(memory_space=pl.ANY),
                  pl.BlockSpec((1, d, fe), lambda i, te, nu: (te[i], 0, 0)),
                  pl.BlockSpec((1, d, fe), lambda i, te, nu: (te[i], 0, 0)),
                  pl.BlockSpec((1, fe, d), lambda i, te, nu: (te[i], 0, 0))],
        out_specs=pl.BlockSpec((tm,) + ROW_TILE, lambda i, te, nu: (i, 0, 0)),
        scratch_shapes=[pltpu.VMEM((2, tm) + ROW_TILE, F32), pltpu.SemaphoreType.DMA((2,))])
    return pl.pallas_call(
        functools.partial(_moe_ffn_body, tm=tm),
        grid_spec=grid_spec,
        out_shape=jax.ShapeDtypeStruct((n_tiles * tm,) + ROW_TILE, F32),
        compiler_params=_cparams("arbitrary"),
        name="moe_ffn",
    )(tile_expert, n_used, row_token, row_token, h, w_gate.astype(BF16), w_up.astype(BF16), w_down.astype(BF16))


def _moe_combine_body(cur_ref, nxt_ref, x_ref, w_ref, y_hbm, o_ref, ybuf, sem, *, tc):
    i = pl.program_id(0)
    slot = i % 2
    n = TOP_K * tc

    @pl.when(i == 0)
    def _():
        _gather_rows(cur_ref, y_hbm, ybuf.at[0], sem.at[0], 0, n)

    @pl.when(i + 1 < pl.num_programs(0))
    def _():
        _gather_rows(nxt_ref, y_hbm, ybuf.at[1 - slot], sem.at[1 - slot], 0, n)

    _wait_rows(y_hbm, ybuf.at[slot], sem.at[slot], n)
    w = w_ref[...]
    y0 = _load_row_tiles(ybuf.at[slot, pl.ds(0, tc)])
    y1 = _load_row_tiles(ybuf.at[slot, pl.ds(tc, tc)])
    o_ref[...] = x_ref[...] + w[:, 0:1] * y0 + w[:, 1:2] * y1


def _moe_combine(x2d, ys, dest, w):
    t, d = x2d.shape
    tc = _tile(t, TM_COMBINE)
    nt = t // tc
    dest_tiles = dest.reshape(nt, tc, TOP_K).transpose(0, 2, 1).reshape(nt, 1, TOP_K * tc)
    return pl.pallas_call(
        functools.partial(_moe_combine_body, tc=tc),
        grid=(nt,),
        in_specs=[pl.BlockSpec((1, 1, TOP_K * tc), lambda i: (i, 0, 0), memory_space=pltpu.SMEM),
                  pl.BlockSpec((1, 1, TOP_K * tc), lambda i: (jnp.minimum(i + 1, nt - 1), 0, 0),
                               memory_space=pltpu.SMEM),
                  pl.BlockSpec((tc, d), lambda i: (i, 0)),
                  pl.BlockSpec((tc, TOP_K), lambda i: (i, 0)),
                  pl.BlockSpec(memory_space=pl.ANY)],
        out_specs=pl.BlockSpec((tc, d), lambda i: (i, 0)),
        out_shape=jax.ShapeDtypeStruct((t, d), F32),
        scratch_shapes=[pltpu.VMEM((2, TOP_K * tc) + ROW_TILE, F32), pltpu.SemaphoreType.DMA((2,))],
        compiler_params=_cparams("arbitrary"),
        name="moe_combine",
    )(dest_tiles, dest_tiles, x2d, w, ys)


def _moe(x2d, g, w_router, w_gate, w_up, w_down):
    h, idx, w = _moe_route(x2d, g, w_router)
    row_token, dest, tile_expert, n_used = _moe_plan(idx, _tile(TOP_K * x2d.shape[0], TM_MOE))
    ys = _moe_ffn(h, row_token, tile_expert, n_used, w_gate, w_up, w_down)
    return _moe_combine(x2d, ys, dest, w)


def _permute_w_in(w):
    off = np.concatenate([[0], np.cumsum(IN_SPLITS)]).tolist()
    piece = lambda i: w[:, off[i]:off[i + 1]]
    order = [9, 0, 1, 3, 4, 5, 7, 8, 2, 6]
    pad = jnp.zeros((w.shape[0], D_PROJ - off[-1]), w.dtype)
    return jnp.concatenate([piece(i) for i in order] + [pad], axis=1).astype(BF16)


def kernel(x, positions, g_mix, w_in, g_cq, w_uq, g_ckv, w_ukv, g_qk_q, g_qk_k, w_gla_gate_up, b_gla_gate, g_gla_out, w_pool, pool_scale, w_branch, w_out, g_ffn, w_ffn_gate, w_ffn_up, w_ffn_down, w_router, w_exp_gate, w_exp_up, w_exp_down):
    b, s, d = x.shape
    depth = g_mix.shape[0]
    for layer in range(depth):
        proj = _norm_matmul(x.reshape(b * s, d), g_mix[layer], _permute_w_in(w_in[layer]), BF16).reshape(b, s, D_PROJ)
        qT, k, vT = _mla_prep(proj, positions, g_cq[layer], w_uq[layer], g_ckv[layer], w_ukv[layer],
                              g_qk_q[layer], g_qk_k[layer])
        yaT = _attention(qT, k, vT).reshape(b, MLA_HEADS * MLA_V, s)
        yb = _gla(proj, w_gla_gate_up[layer], b_gla_gate[layer], g_gla_out[layer])
        yc = _pool(proj, w_pool[layer], pool_scale[layer])
        x = _merge(x, yaT, yb, yc, proj, w_branch[layer], w_out[layer])
        i = layer // 2
        if layer % 2 == 0:
            x2d = _ffn(x.reshape(b * s, d), g_ffn[layer], w_ffn_gate[i], w_ffn_up[i], w_ffn_down[i])
        else:
            x2d = _moe(x.reshape(b * s, d), g_ffn[layer], w_router[i], w_exp_gate[i], w_exp_up[i], w_exp_down[i])
        x = x2d.reshape(b, s, d)
    return x
```

```python
import functools

import jax
import jax.numpy as jnp
import numpy as np
from jax import lax
from jax.experimental import pallas as pl
from jax.experimental.pallas import tpu as pltpu

F32 = jnp.float32
BF16 = jnp.bfloat16
HIGHEST = lax.Precision.HIGHEST

D_MODEL = 1024
MLA_HEADS = 8
MLA_NOPE = 64
MLA_ROPE = 32
MLA_QK = MLA_NOPE + MLA_ROPE
MLA_V = 64
MLA_Q_RANK = D_MODEL // 4
MLA_KV_RANK = D_MODEL // 4
ROPE_BASE = 10000.0
GLA_HEADS = 4
GLA_DK = 64
GLA_DV = 128
GLA_GATE_RANK = 16
GLA_TAU = 16.0
POOL_WINDOWS = (2, 4, 8, 16)
POOL_GROUP = 128
POOL_WIDTH = 4 * POOL_GROUP
N_BRANCH = 3
BRANCH_WIDTH = 512
D_FF = 2816
N_EXPERTS = 8
TOP_K = 2
D_EXPERT = 1408
EPS = 1e-6

IN_SPLITS = (MLA_Q_RANK, MLA_KV_RANK, MLA_ROPE,
             GLA_HEADS * GLA_DK, GLA_HEADS * GLA_DK, GLA_HEADS * GLA_DV, GLA_GATE_RANK, GLA_HEADS * GLA_DV,
             POOL_WIDTH, N_BRANCH * D_MODEL)

LANES = 128
HEAD_PAD = LANES

COL_GATES = 0
COL_CQ = 3072
COL_CKV = 3328
COL_GQ = 3584
COL_GK = 3840
COL_GV = 4096
COL_GOG = 4608
COL_POOL = 5120
COL_SMALL = 5632
D_PROJ = 5760
SMALL_LR = MLA_ROPE

TM_PROJ = 1024
TN_PROJ = 1920
ATT_T = 512
ATT_NH = 4
V_ROWS = 80
GLA_L = 128
GLA_BASE = 1
GLA_NB = 4
TM_POOL = 256
POOL_HALO = 16
TM_MERGE = 512
TM_FFN = 1024
TF_FFN = 256
TM_ROUTE = 1024
TM_MOE = 512
TM_COMBINE = 512
VMEM_LIMIT = 56 * 1024 * 1024


def _cparams(*sem):
    return pltpu.CompilerParams(dimension_semantics=sem, vmem_limit_bytes=VMEM_LIMIT)


def _tile(n, pref):
    t = min(n, pref)
    assert n % t == 0, (n, pref)
    return t


def _rms(x):
    return x * lax.rsqrt(jnp.mean(x * x, axis=-1, keepdims=True) + EPS)


_NT = (((1,), (1,)), ((), ()))
_TN = (((0,), (0,)), ((), ()))


def _norm_matmul_body(x_ref, g_ref, w_ref, o_ref, h_ref):
    @pl.when(pl.program_id(1) == 0)
    def _():
        h_ref[...] = (_rms(x_ref[...]) * g_ref[...]).astype(BF16)

    o_ref[...] = jnp.dot(h_ref[...], w_ref[...], preferred_element_type=F32).astype(o_ref.dtype)


def _norm_matmul(x2d, g, w, out_dtype):
    t, d = x2d.shape
    n = w.shape[1]
    tm, tn = _tile(t, TM_PROJ), _tile(n, TN_PROJ)
    return pl.pallas_call(
        _norm_matmul_body,
        grid=(t // tm, n // tn),
        in_specs=[pl.BlockSpec((tm, d), lambda i, j: (i, 0)),
                  pl.BlockSpec((1, d), lambda i, j: (0, 0)),
                  pl.BlockSpec((d, tn), lambda i, j: (0, j))],
        out_specs=pl.BlockSpec((tm, tn), lambda i, j: (i, j)),
        out_shape=jax.ShapeDtypeStruct((t, n), out_dtype),
        scratch_shapes=[pltpu.VMEM((tm, d), BF16)],
        compiler_params=_cparams("parallel", "arbitrary"),
        name="norm_in_proj",
    )(x2d, g.reshape(1, d), w)


def _mla_prep_body(pos_ref, cq_ref, ckv_ref, small_ref, gcq_ref, wuq_ref, gckv_ref, wukv_ref, gq_ref, gk_ref,
                   invf_ref, qT_ref, k_ref, vT_ref):
    tp = cq_ref.shape[1]
    half = MLA_ROPE // 2
    ang = invf_ref[...] * pos_ref[0].astype(F32)
    cos, sin = jnp.cos(ang), jnp.sin(ang)

    def rope(t):
        x1, x2 = t[:half], t[half:]
        return x1 * cos - x2 * sin, x2 * cos + x1 * sin

    cqn = (_rms(cq_ref[0].astype(F32)) * gcq_ref[...]).astype(BF16)
    ckvn = (_rms(ckv_ref[0].astype(F32)) * gckv_ref[...]).astype(BF16)
    qT = lax.dot_general(wuq_ref[...], cqn, _NT, preferred_element_type=F32)
    kvT = lax.dot_general(wukv_ref[...], ckvn, _NT, preferred_element_type=F32)
    kr = small_ref[0].astype(F32).T[:MLA_ROPE]
    kr_ss = jnp.sum(kr * kr, axis=0, keepdims=True)
    gq, gk = gq_ref[...], gk_ref[...]
    zpad = jnp.zeros((HEAD_PAD - MLA_QK, tp), F32)
    vrow = lax.broadcasted_iota(jnp.int32, (V_ROWS - MLA_V, tp), 0)
    vpad = jnp.where(vrow == 0, 1.0, 0.0).astype(F32)
    scale = MLA_QK ** -0.5 * np.log2(np.e)
    for h in range(MLA_HEADS):
        qh = qT[h * MLA_QK:(h + 1) * MLA_QK]
        qn = qh * lax.rsqrt(jnp.mean(qh * qh, axis=0, keepdims=True) + EPS) * (gq * scale)
        r1, r2 = rope(qn[MLA_NOPE:])
        qT_ref[0, h] = jnp.concatenate([qn[:MLA_NOPE], r1, r2, zpad], axis=0).astype(BF16)
        base = h * (MLA_NOPE + MLA_V)
        kn = kvT[base:base + MLA_NOPE]
        r = lax.rsqrt((jnp.sum(kn * kn, axis=0, keepdims=True) + kr_ss) * (1.0 / MLA_QK) + EPS)
        r1, r2 = rope(kr * r * gk[MLA_NOPE:])
        kfm = jnp.concatenate([kn * r * gk[:MLA_NOPE], r1, r2, zpad], axis=0)
        k_ref[0, h] = kfm.T.astype(BF16)
        vT_ref[0, h, 0] = jnp.concatenate([kvT[base + MLA_NOPE:base + MLA_NOPE + MLA_V], vpad], axis=0).astype(BF16)


def _mla_prep(proj, positions, g_cq, w_uq, g_ckv, w_ukv, g_qk_q, g_qk_k):
    b, s, _ = proj.shape
    tp = _tile(s, ATT_T)
    ns = s // tp
    h = MLA_HEADS
    pos = positions.reshape(b * ns, 1, tp)
    inv_freq = (ROPE_BASE ** (-jnp.arange(MLA_ROPE // 2, dtype=F32) / (MLA_ROPE // 2))).reshape(-1, 1)
    const = lambda shape: pl.BlockSpec(shape, lambda bi, si: (0,) * len(shape))
    return pl.pallas_call(
        _mla_prep_body,
        grid=(b, ns),
        in_specs=[pl.BlockSpec((1, 1, tp), lambda bi, si: (bi * ns + si, 0, 0)),
                  pl.BlockSpec((1, tp, MLA_Q_RANK), lambda bi, si: (bi, si, COL_CQ // MLA_Q_RANK)),
                  pl.BlockSpec((1, tp, MLA_KV_RANK), lambda bi, si: (bi, si, COL_CKV // MLA_KV_RANK)),
                  pl.BlockSpec((1, tp, LANES), lambda bi, si: (bi, si, COL_SMALL // LANES)),
                  const((1, MLA_Q_RANK)), const((h * MLA_QK, MLA_Q_RANK)),
                  const((1, MLA_KV_RANK)), const((h * (MLA_NOPE + MLA_V), MLA_KV_RANK)),
                  const((MLA_QK, 1)), const((MLA_QK, 1)), const((MLA_ROPE // 2, 1))],
        out_specs=[pl.BlockSpec((1, h, HEAD_PAD, tp), lambda bi, si: (bi, 0, 0, si)),
                   pl.BlockSpec((1, h, tp, HEAD_PAD), lambda bi, si: (bi, 0, si, 0)),
                   pl.BlockSpec((1, h, 1, V_ROWS, tp), lambda bi, si: (bi, 0, si, 0, 0))],
        out_shape=[jax.ShapeDtypeStruct((b, h, HEAD_PAD, s), BF16),
                   jax.ShapeDtypeStruct((b, h, s, HEAD_PAD), BF16),
                   jax.ShapeDtypeStruct((b, h, ns, V_ROWS, tp), BF16)],
        compiler_params=_cparams("parallel", "parallel"),
        name="mla_prep",
    )(pos, proj, proj, proj, g_cq.reshape(1, -1), w_uq.T.astype(BF16), g_ckv.reshape(1, -1), w_ukv.T.astype(BF16),
      g_qk_q.reshape(-1, 1), g_qk_k.reshape(-1, 1), inv_freq)


def _attn_body(qT_ref, k_ref, vT_ref, o_ref, *, tq, nh):
    i = pl.program_id(2)

    def step(j, carry, masked):
        scores = []
        for h in range(nh):
            kb = k_ref[0, h, pl.ds(pl.multiple_of(j * tq, tq), tq), :]
            s = jnp.dot(kb, qT_ref[0, h], preferred_element_type=F32)
            if masked:
                kpos = lax.broadcasted_iota(jnp.int32, (tq, tq), 0)
                qpos = lax.broadcasted_iota(jnp.int32, (tq, tq), 1)
                s = jnp.where(kpos <= qpos, s, -jnp.inf)
            scores.append(s)
        out = []
        for h in range(nh):
            m, acc = carry[h]
            m_new = jnp.maximum(m, jnp.max(scores[h], axis=0, keepdims=True))
            p = jnp.exp2(scores[h] - m_new).astype(BF16)
            acc = jnp.exp2(m - m_new) * acc + jnp.dot(vT_ref[0, h, j], p, preferred_element_type=F32)
            out.append((m_new, acc))
        return tuple(out)

    carry = tuple((jnp.full((1, tq), -jnp.inf, F32), jnp.zeros((V_ROWS, tq), F32)) for _ in range(nh))
    carry = lax.fori_loop(0, i, functools.partial(step, masked=False), carry)
    carry = step(i, carry, True)
    for h in range(nh):
        acc = carry[h][1]
        o_ref[0, h] = (acc[:MLA_V] / acc[MLA_V:MLA_V + 1]).astype(o_ref.dtype)


def _attention(qT, k, vT):
    b, h, _, s = qT.shape
    tq = vT.shape[-1]
    nk = s // tq
    nh = ATT_NH
    return pl.pallas_call(
        functools.partial(_attn_body, tq=tq, nh=nh),
        grid=(b, h // nh, s // tq),
        in_specs=[pl.BlockSpec((1, nh, HEAD_PAD, tq), lambda bi, hi, qi: (bi, hi, 0, qi)),
                  pl.BlockSpec((1, nh, s, HEAD_PAD), lambda bi, hi, qi: (bi, hi, 0, 0)),
                  pl.BlockSpec((1, nh, nk, V_ROWS, tq), lambda bi, hi, qi: (bi, hi, 0, 0, 0))],
        out_specs=pl.BlockSpec((1, nh, MLA_V, tq), lambda bi, hi, qi: (bi, hi, 0, qi)),
        out_shape=jax.ShapeDtypeStruct((b, h, MLA_V, s), BF16),
        compiler_params=_cparams("parallel", "parallel", "arbitrary"),
        name="mla_attention",
    )(qT, k, vT)


def _gla_body(q_ref, k_ref, v_ref, og_ref, small_ref, wg_ref, bg_ref, gout_ref, o_ref, st_ref, *, blk):
    @pl.when(pl.program_id(1) == 0)
    def _():
        st_ref[...] = jnp.zeros_like(st_ref)

    hd = GLA_HEADS * GLA_DK
    nb = q_ref.shape[0]
    row = lax.broadcasted_iota(jnp.int32, (blk, blk), 0)
    col = lax.broadcasted_iota(jnp.int32, (blk, blk), 1)
    tril = (col <= row).astype(F32)

    def prepare(bb):
        lr = small_ref[bb][:, SMALL_LR:SMALL_LR + GLA_GATE_RANK].astype(F32)
        z = jnp.dot(lr, wg_ref[...], precision=HIGHEST, preferred_element_type=F32) + bg_ref[...]
        log_a = jax.nn.log_sigmoid(z) * (1.0 / GLA_TAU)
        bc = jnp.dot(tril, log_a, precision=HIGHEST, preferred_element_type=F32)
        q = q_ref[bb].astype(F32) * (GLA_DK ** -0.5)
        k = k_ref[bb].astype(F32)

        def ref_rows(group, off):
            parts = [jnp.broadcast_to(bc[g0 + off:g0 + off + 1], (group, hd)) for g0 in range(0, blk, group)]
            return parts[0] if len(parts) == 1 else jnp.concatenate(parts, axis=0)

        levels = []
        group = blk
        while group > GLA_BASE:
            half = group // 2
            ref = ref_rows(group, half - 1)
            qs = (q * jnp.exp(jnp.minimum(bc - ref, 0.0))).astype(BF16)
            ks = (k * jnp.exp(jnp.minimum(ref - bc, 0.0))).astype(BF16)
            mask = ((row & -group) == (col & -group)) & ((row & half) != 0) & ((col & half) == 0)
            levels.append((qs, ks, mask))
            group = half
        if GLA_BASE == 1:
            levels.append((q.astype(BF16), k.astype(BF16), row == col))
        else:
            ref = ref_rows(GLA_BASE, GLA_BASE // 2 - 1)
            mask = ((row & -GLA_BASE) == (col & -GLA_BASE)) & (col <= row)
            levels.append(((q * jnp.exp(bc - ref)).astype(BF16), (k * jnp.exp(ref - bc)).astype(BF16), mask))
        b_last = bc[blk - 1:blk]
        return dict(levels=levels, qd=(q * jnp.exp(bc)).astype(BF16), kd=(k * jnp.exp(b_last - bc)).astype(BF16),
                    a_last=jnp.exp(b_last))

    prep = [prepare(bb) for bb in range(nb)]
    gout = gout_ref[...]
    for h in range(GLA_HEADS):
        ks_, vs_ = slice(h * GLA_DK, (h + 1) * GLA_DK), slice(h * GLA_DV, (h + 1) * GLA_DV)
        for bb in range(nb):
            pr = prep[bb]
            attn = jnp.zeros((blk, blk), F32)
            for qs, ksc, mask in pr["levels"]:
                p = lax.dot_general(qs[:, ks_], ksc[:, ks_], _NT, preferred_element_type=F32)
                attn = jnp.where(mask, p, attn)
            v_h = v_ref[bb, :, vs_]
            st = st_ref[bb, h]
            o = jnp.dot(attn.astype(BF16), v_h, preferred_element_type=F32)
            o = o + lax.dot_general(pr["qd"][:, ks_], st.astype(BF16), _NT, preferred_element_type=F32)
            st_ref[bb, h] = (st * pr["a_last"][:, ks_]
                             + lax.dot_general(v_h, pr["kd"][:, ks_], _TN, preferred_element_type=F32))
            og = og_ref[bb, :, vs_].astype(F32)
            o_ref[bb, :, vs_] = (_rms(o) * gout * (og * jax.nn.sigmoid(og))).astype(o_ref.dtype)


def _gla(proj, w_gate_up, b_gate, g_out):
    b, s, _ = proj.shape
    blk = _tile(s, GLA_L)
    nb = _tile(b, GLA_NB)
    hd, hv = GLA_HEADS * GLA_DK, GLA_HEADS * GLA_DV
    const = lambda shape: pl.BlockSpec(shape, lambda bi, si: (0,) * len(shape))
    return pl.pallas_call(
        functools.partial(_gla_body, blk=blk),
        grid=(b // nb, s // blk),
        in_specs=[pl.BlockSpec((nb, blk, hd), lambda bi, si: (bi, si, COL_GQ // hd)),
                  pl.BlockSpec((nb, blk, hd), lambda bi, si: (bi, si, COL_GK // hd)),
                  pl.BlockSpec((nb, blk, hv), lambda bi, si: (bi, si, COL_GV // hv)),
                  pl.BlockSpec((nb, blk, hv), lambda bi, si: (bi, si, COL_GOG // hv)),
                  pl.BlockSpec((nb, blk, LANES), lambda bi, si: (bi, si, COL_SMALL // LANES)),
                  const((GLA_GATE_RANK, hd)), const((1, hd)), const((1, GLA_DV))],
        out_specs=pl.BlockSpec((nb, blk, hv), lambda bi, si: (bi, si, 0)),
        out_shape=jax.ShapeDtypeStruct((b, s, hv), BF16),
        scratch_shapes=[pltpu.VMEM((nb, GLA_HEADS, GLA_DV, GLA_DK), F32)],
        compiler_params=_cparams("parallel", "arbitrary"),
        name="gla",
    )(proj, proj, proj, proj, proj, w_gate_up, b_gate.reshape(1, -1), g_out.reshape(1, -1))


def _pool_body(u_ref, halo_ref, w_ref, scale_ref, o_ref, *, tm):
    si = pl.program_id(1)
    row = lax.broadcasted_iota(jnp.int32, (tm, tm + POOL_HALO), 0)
    col = lax.broadcasted_iota(jnp.int32, (tm, tm + POOL_HALO), 1)
    t1 = (si * tm + lax.broadcasted_iota(jnp.int32, (tm, 1), 0) + 1).astype(F32)
    for g, win in enumerate(POOL_WINDOWS):
        cs = slice(g * POOL_GROUP, (g + 1) * POOL_GROUP)
        cur = u_ref[0, :, cs]
        halo = halo_ref[0, :, cs]
        ext = jnp.concatenate([jnp.where(si > 0, halo, jnp.zeros_like(halo)), cur], axis=0)
        band = ((col <= row + POOL_HALO) & (col > row + POOL_HALO - win)).astype(BF16)
        wsum = jnp.dot(band, ext, preferred_element_type=F32)
        pooled = wsum / jnp.minimum(t1, float(win)) - cur.astype(F32)
        y = jnp.dot(pooled.astype(BF16), w_ref[g], preferred_element_type=F32)
        o_ref[0, :, cs] = (y * scale_ref[:, cs]).astype(o_ref.dtype)


def _pool(proj, w_pool, pool_scale):
    b, s, _ = proj.shape
    tm = _tile(s, TM_POOL)
    hb = tm // POOL_HALO
    return pl.pallas_call(
        functools.partial(_pool_body, tm=tm),
        grid=(b, s // tm),
        in_specs=[pl.BlockSpec((1, tm, POOL_WIDTH), lambda bi, si: (bi, si, COL_POOL // POOL_WIDTH)),
                  pl.BlockSpec((1, POOL_HALO, POOL_WIDTH),
                               lambda bi, si: (bi, jnp.maximum(si * hb - 1, 0), COL_POOL // POOL_WIDTH)),
                  pl.BlockSpec((len(POOL_WINDOWS), POOL_GROUP, POOL_GROUP), lambda bi, si: (0, 0, 0)),
                  pl.BlockSpec((1, POOL_WIDTH), lambda bi, si: (0, 0))],
        out_specs=pl.BlockSpec((1, tm, POOL_WIDTH), lambda bi, si: (bi, si, 0)),
        out_shape=jax.ShapeDtypeStruct((b, s, POOL_WIDTH), BF16),
        compiler_params=_cparams("parallel", "parallel"),
        name="pool",
    )(proj, proj, w_pool.astype(BF16), pool_scale.reshape(1, -1))


def _merge_body(x_ref, yaT_ref, yb_ref, yc_ref, g0_ref, g1_ref, g2_ref, wb_ref, wo_ref, o_ref):
    d_a = lax.dot_general(yaT_ref[0], wb_ref[0], _TN, preferred_element_type=F32)
    d_b = jnp.dot(yb_ref[0], wb_ref[1], preferred_element_type=F32)
    d_c = jnp.dot(yc_ref[0], wb_ref[2], preferred_element_type=F32)
    merged = (jax.nn.sigmoid(g0_ref[0].astype(F32)) * d_a + jax.nn.sigmoid(g1_ref[0].astype(F32)) * d_b
              + jax.nn.sigmoid(g2_ref[0].astype(F32)) * d_c)
    o_ref[0] = x_ref[0] + jnp.dot(merged.astype(BF16), wo_ref[...], preferred_element_type=F32)


def _merge(x, yaT, yb, yc, proj, w_branch, w_out):
    b, s, d = x.shape
    tm = _tile(s, TM_MERGE)
    tok = lambda width, cb: pl.BlockSpec((1, tm, width), lambda bi, si: (bi, si, cb))
    return pl.pallas_call(
        _merge_body,
        grid=(b, s // tm),
        in_specs=[tok(d, 0),
                  pl.BlockSpec((1, BRANCH_WIDTH, tm), lambda bi, si: (bi, 0, si)),
                  tok(BRANCH_WIDTH, 0), tok(BRANCH_WIDTH, 0),
                  tok(d, COL_GATES // d), tok(d, COL_GATES // d + 1), tok(d, COL_GATES // d + 2),
                  pl.BlockSpec((N_BRANCH, BRANCH_WIDTH, d), lambda bi, si: (0, 0, 0)),
                  pl.BlockSpec((d, d), lambda bi, si: (0, 0))],
        out_specs=tok(d, 0),
        out_shape=jax.ShapeDtypeStruct((b, s, d), F32),
        compiler_params=_cparams("parallel", "parallel"),
        name="merge_out_proj",
    )(x, yaT, yb, yc, proj, proj, proj, w_branch.astype(BF16), w_out.astype(BF16))


def _ffn_body(x_ref, g_ref, wg_ref, wu_ref, wd_ref, o_ref, h_ref, act_ref):
    f = pl.program_id(1)
    nf = act_ref.shape[0]

    @pl.when(f == 0)
    def _():
        h_ref[...] = (_rms(x_ref[...]) * g_ref[...]).astype(BF16)

    h = h_ref[...]
    a = jnp.dot(h, wg_ref[...], preferred_element_type=F32)
    u = jnp.dot(h, wu_ref[...], preferred_element_type=F32)
    act_ref[f] = (a * jax.nn.sigmoid(a) * u).astype(BF16)

    @pl.when(f == nf - 1)
    def _():
        act = jnp.concatenate([act_ref[c] for c in range(nf)], axis=1)
        o_ref[...] = x_ref[...] + jnp.dot(act, wd_ref[...], preferred_element_type=F32)


def _ffn(x2d, g, w_gate, w_up, w_down):
    t, d = x2d.shape
    ff = w_gate.shape[1]
    tm, tf = _tile(t, TM_FFN), _tile(ff, TF_FFN)
    return pl.pallas_call(
        _ffn_body,
        grid=(t // tm, ff // tf),
        in_specs=[pl.BlockSpec((tm, d), lambda i, f: (i, 0)),
                  pl.BlockSpec((1, d), lambda i, f: (0, 0)),
                  pl.BlockSpec((d, tf), lambda i, f: (0, f)),
                  pl.BlockSpec((d, tf), lambda i, f: (0, f)),
                  pl.BlockSpec((ff, d), lambda i, f: (0, 0))],
        out_specs=pl.BlockSpec((tm, d), lambda i, f: (i, 0)),
        out_shape=jax.ShapeDtypeStruct((t, d), F32),
        scratch_shapes=[pltpu.VMEM((tm, d), BF16), pltpu.VMEM((ff // tf, tm, tf), BF16)],
        compiler_params=_cparams("parallel", "arbitrary"),
        name="ffn_dense",
    )(x2d, g.reshape(1, d), w_gate.astype(BF16), w_up.astype(BF16), w_down.astype(BF16))


ROW_TILE = (8, LANES)


def _store_row_tiles(ref, val):
    for c in range(ROW_TILE[0]):
        ref[:, c, :] = val[:, c * LANES:(c + 1) * LANES]


def _load_row_tiles(ref_view):
    return jnp.concatenate([ref_view[:, c, :] for c in range(ROW_TILE[0])], axis=1)


def _route_body(x_ref, g_ref, wr_ref, h_ref, idx_ref, w_ref):
    hf = _rms(x_ref[...]) * g_ref[...]
    _store_row_tiles(h_ref, hf)
    logits = jnp.dot(hf, wr_ref[...], precision=HIGHEST, preferred_element_type=F32)
    lane = lax.broadcasted_iota(jnp.int32, logits.shape, 1)
    m1 = jnp.max(logits, axis=-1, keepdims=True)
    i1 = jnp.min(jnp.where(logits == m1, lane, N_EXPERTS), axis=-1, keepdims=True)
    rest = jnp.where(lane == i1, -jnp.inf, logits)
    m2 = jnp.max(rest, axis=-1, keepdims=True)
    i2 = jnp.min(jnp.where(rest == m2, lane, N_EXPERTS), axis=-1, keepdims=True)
    e2 = jnp.exp(m2 - m1)
    w1 = 1.0 / (1.0 + e2)
    first = lax.broadcasted_iota(jnp.int32, idx_ref.shape, 1) == 0
    idx_ref[...] = jnp.where(first, i1, i2)
    w_ref[...] = jnp.where(first, w1, e2 * w1)


def _moe_route(x2d, g, w_router):
    t, d = x2d.shape
    ne = w_router.shape[1]
    tm = _tile(t, TM_ROUTE)
    return pl.pallas_call(
        _route_body,
        grid=(t // tm,),
        in_specs=[pl.BlockSpec((tm, d), lambda i: (i, 0)),
                  pl.BlockSpec((1, d), lambda i: (0, 0)),
                  pl.BlockSpec((d, ne), lambda i: (0, 0))],
        out_specs=[pl.BlockSpec((tm,) + ROW_TILE, lambda i: (i, 0, 0)),
                   pl.BlockSpec((tm, TOP_K), lambda i: (i, 0)),
                   pl.BlockSpec((tm, TOP_K), lambda i: (i, 0))],
        out_shape=[jax.ShapeDtypeStruct((t,) + ROW_TILE, F32),
                   jax.ShapeDtypeStruct((t, TOP_K), jnp.int32),
                   jax.ShapeDtypeStruct((t, TOP_K), F32)],
        compiler_params=_cparams("parallel"),
        name="moe_route",
    )(x2d, g.reshape(1, d), w_router)


def _moe_plan(idx, tm):
    t = idx.shape[0]
    e = idx.reshape(-1)
    onehot = (e[:, None] == jnp.arange(N_EXPERTS, dtype=jnp.int32)[None, :]).astype(jnp.int32)
    csum = jnp.cumsum(onehot, axis=0)
    rank = jnp.sum((csum - onehot) * onehot, axis=1)
    tiles = (csum[-1] + tm - 1) // tm
    tile_end = jnp.cumsum(tiles)
    row0 = (tile_end - tiles) * tm
    dest = jnp.sum(onehot * row0[None, :], axis=1) + rank
    n_tiles = (TOP_K * t) // tm + N_EXPERTS
    row_token = jnp.zeros((n_tiles * tm,), jnp.int32).at[dest].set(
        jnp.arange(TOP_K * t, dtype=jnp.int32) // TOP_K, unique_indices=True)
    tile_expert = jnp.sum(jnp.arange(n_tiles, dtype=jnp.int32)[:, None] >= tile_end[None, :], axis=1)
    tile_expert = jnp.minimum(tile_expert, N_EXPERTS - 1).astype(jnp.int32)
    return row_token.reshape(n_tiles, 1, tm), dest.reshape(t, TOP_K), tile_expert, tile_end[-1:].astype(jnp.int32)


def _gather_rows(idx_ref, src_hbm, buf, sem, lo, n):
    def body(p, carry):
        for q in range(2):
            r = lo + 2 * p + q
            pltpu.make_async_copy(src_hbm.at[pl.ds(idx_ref[0, 0, r], 1)], buf.at[pl.ds(r, 1)], sem).start(priority=q)
        return carry

    lax.fori_loop(0, n // 2, body, 0)


def _wait_rows(src_hbm, buf, sem, n):
    pltpu.make_async_copy(src_hbm.at[pl.ds(0, n)], buf, sem).wait()


def _moe_ffn_body(te_ref, nu_ref, cur_ref, nxt_ref, h_hbm, wg_ref, wu_ref, wd_ref, o_ref, xbuf, sem, *, tm):
    i = pl.program_id(0)
    slot = i % 2
    n_used = nu_ref[0]

    @pl.when(jnp.logical_and(i == 0, n_used > 0))
    def _():
        _gather_rows(cur_ref, h_hbm, xbuf.at[0], sem.at[0], 0, tm)

    def prefetch(part):
        @pl.when(i + 1 < n_used)
        def _():
            _gather_rows(nxt_ref, h_hbm, xbuf.at[1 - slot], sem.at[1 - slot], part * (tm // 4), tm // 4)

    @pl.when(i < n_used)
    def _():
        _wait_rows(h_hbm, xbuf.at[slot], sem.at[slot], tm)
        xs = _load_row_tiles(xbuf.at[slot]).astype(BF16)
        prefetch(0)
        a = jnp.dot(xs, wg_ref[0], preferred_element_type=F32)
        prefetch(1)
        u = jnp.dot(xs, wu_ref[0], preferred_element_type=F32)
        prefetch(2)
        act = (a * jax.nn.sigmoid(a) * u).astype(BF16)
        y = jnp.dot(act, wd_ref[0], preferred_element_type=F32)
        prefetch(3)
        _store_row_tiles(o_ref, y)

    @pl.when(i >= n_used)
    def _():
        o_ref[...] = jnp.zeros_like(o_ref)


def _moe_ffn(h, row_token, tile_expert, n_used, w_gate, w_up, w_down):
    ne, d, fe = w_gate.shape
    assert h.shape[1:] == ROW_TILE and d == ROW_TILE[0] * ROW_TILE[1]
    n_tiles, _, tm = row_token.shape
    grid_spec = pltpu.PrefetchScalarGridSpec(
        num_scalar_prefetch=2,
        grid=(n_tiles,),
        in_specs=[pl.BlockSpec((1, 1, tm), lambda i, te, nu: (i, 0, 0), memory_space=pltpu.SMEM),
                  pl.BlockSpec((1, 1, tm), lambda i, te, nu: (jnp.minimum(i + 1, n_tiles - 1), 0, 0),
                               memory_space=pltpu.SMEM),
                  pl.BlockSpec(memory_space=pl.ANY),
                  pl.BlockSpec((1, d, fe), lambda i, te, nu: (te[i], 0, 0)),
                  pl.BlockSpec((1, d, fe), lambda i, te, nu: (te[i], 0, 0)),
                  pl.BlockSpec((1, fe, d), lambda i, te, nu: (te[i], 0, 0))],
        out_specs=pl.BlockSpec((tm,) + ROW_TILE, lambda i, te, nu: (i, 0, 0)),
        scratch_shapes=[pltpu.VMEM((2, tm) + ROW_TILE, F32), pltpu.SemaphoreType.DMA((2,))])
    return pl.pallas_call(
        functools.partial(_moe_ffn_body, tm=tm),
        grid_spec=grid_spec,
        out_shape=jax.ShapeDtypeStruct((n_tiles * tm,) + ROW_TILE, F32),
        compiler_params=_cparams("arbitrary"),
        name="moe_ffn",
    )(tile_expert, n_used, row_token, row_token, h, w_gate.astype(BF16), w_up.astype(BF16), w_down.astype(BF16))


def _moe_combine_body(cur_ref, nxt_ref, x_ref, w_ref, y_hbm, o_ref, ybuf, sem, *, tc):
    i = pl.program_id(0)
    slot = i % 2
    n = TOP_K * tc

    @pl.when(i == 0)
    def _():
        _gather_rows(cur_ref, y_hbm, ybuf.at[0], sem.at[0], 0, n)

    @pl.when(i + 1 < pl.num_programs(0))
    def _():
        _gather_rows(nxt_ref, y_hbm, ybuf.at[1 - slot], sem.at[1 - slot], 0, n)

    _wait_rows(y_hbm, ybuf.at[slot], sem.at[slot], n)
    w = w_ref[...]
    y0 = _load_row_tiles(ybuf.at[slot, pl.ds(0, tc)])
    y1 = _load_row_tiles(ybuf.at[slot, pl.ds(tc, tc)])
    o_ref[...] = x_ref[...] + w[:, 0:1] * y0 + w[:, 1:2] * y1


def _moe_combine(x2d, ys, dest, w):
    t, d = x2d.shape
    tc = _tile(t, TM_COMBINE)
    nt = t // tc
    dest_tiles = dest.reshape(nt, tc, TOP_K).transpose(0, 2, 1).reshape(nt, 1, TOP_K * tc)
    return pl.pallas_call(
        functools.partial(_moe_combine_body, tc=tc),
        grid=(nt,),
        in_specs=[pl.BlockSpec((1, 1, TOP_K * tc), lambda i: (i, 0, 0), memory_space=pltpu.SMEM),
                  pl.BlockSpec((1, 1, TOP_K * tc), lambda i: (jnp.minimum(i + 1, nt - 1), 0, 0),
                               memory_space=pltpu.SMEM),
                  pl.BlockSpec((tc, d), lambda i: (i, 0)),
                  pl.BlockSpec((tc, TOP_K), lambda i: (i, 0)),
                  pl.BlockSpec(memory_space=pl.ANY)],
        out_specs=pl.BlockSpec((tc, d), lambda i: (i, 0)),
        out_shape=jax.ShapeDtypeStruct((t, d), F32),
        scratch_shapes=[pltpu.VMEM((2, TOP_K * tc) + ROW_TILE, F32), pltpu.SemaphoreType.DMA((2,))],
        compiler_params=_cparams("arbitrary"),
        name="moe_combine",
    )(dest_tiles, dest_tiles, x2d, w, ys)


def _moe(x2d, g, w_router, w_gate, w_up, w_down):
    h, idx, w = _moe_route(x2d, g, w_router)
    row_token, dest, tile_expert, n_used = _moe_plan(idx, _tile(TOP_K * x2d.shape[0], TM_MOE))
    ys = _moe_ffn(h, row_token, tile_expert, n_used, w_gate, w_up, w_down)
    return _moe_combine(x2d, ys, dest, w)


def _permute_w_in(w):
    off = np.concatenate([[0], np.cumsum(IN_SPLITS)]).tolist()
    piece = lambda i: w[:, off[i]:off[i + 1]].astype(BF16)
    order = [9, 0, 1, 3, 4, 5, 7, 8, 2, 6]
    pad = jnp.zeros((w.shape[0], D_PROJ - off[-1]), BF16)
    return jnp.concatenate([piece(i) for i in order] + [pad], axis=1)


def kernel(x, positions, g_mix, w_in, g_cq, w_uq, g_ckv, w_ukv, g_qk_q, g_qk_k, w_gla_gate_up, b_gla_gate, g_gla_out, w_pool, pool_scale, w_branch, w_out, g_ffn, w_ffn_gate, w_ffn_up, w_ffn_down, w_router, w_exp_gate, w_exp_up, w_exp_down):
    b, s, d = x.shape
    depth = g_mix.shape[0]
    for layer in range(depth):
        proj = _norm_matmul(x.reshape(b * s, d), g_mix[layer], _permute_w_in(w_in[layer]), BF16).reshape(b, s, D_PROJ)
        qT, k, vT = _mla_prep(proj, positions, g_cq[layer], w_uq[layer], g_ckv[layer], w_ukv[layer],
                              g_qk_q[layer], g_qk_k[layer])
        yaT = _attention(qT, k, vT).reshape(b, MLA_HEADS * MLA_V, s)
        yb = _gla(proj, w_gla_gate_up[layer], b_gla_gate[layer], g_gla_out[layer])
        yc = _pool(proj, w_pool[layer], pool_scale[layer])
        x = _merge(x, yaT, yb, yc, proj, w_branch[layer], w_out[layer])
        i = layer // 2
        if layer % 2 == 0:
            x2d = _ffn(x.reshape(b * s, d), g_ffn[layer], w_ffn_gate[i], w_ffn_up[i], w_ffn_down[i])
        else:
            x2d = _moe(x.reshape(b * s, d), g_ffn[layer], w_router[i], w_exp_gate[i], w_exp_up[i], w_exp_down[i])
        x = x2d.reshape(b, s, d)
    return x
```

```python
import functools

import jax
import jax.numpy as jnp
import numpy as np
from jax import lax
from jax.experimental import pallas as pl
from jax.experimental.pallas import tpu as pltpu

F32 = jnp.float32
BF16 = jnp.bfloat16
HIGHEST = lax.Precision.HIGHEST

D_MODEL = 1024
MLA_HEADS = 8
MLA_NOPE = 64
MLA_ROPE = 32
MLA_QK = MLA_NOPE + MLA_ROPE
MLA_V = 64
MLA_Q_RANK = D_MODEL // 4
MLA_KV_RANK = D_MODEL // 4
ROPE_BASE = 10000.0
GLA_HEADS = 4
GLA_DK = 64
GLA_DV = 128
GLA_GATE_RANK = 16
GLA_TAU = 16.0
POOL_WINDOWS = (2, 4, 8, 16)
POOL_GROUP = 128
POOL_WIDTH = 4 * POOL_GROUP
N_BRANCH = 3
BRANCH_WIDTH = 512
D_FF = 2816
N_EXPERTS = 8
TOP_K = 2
D_EXPERT = 1408
EPS = 1e-6

IN_SPLITS = (MLA_Q_RANK, MLA_KV_RANK, MLA_ROPE,
             GLA_HEADS * GLA_DK, GLA_HEADS * GLA_DK, GLA_HEADS * GLA_DV, GLA_GATE_RANK, GLA_HEADS * GLA_DV,
             POOL_WIDTH, N_BRANCH * D_MODEL)

LANES = 128
HEAD_PAD = LANES

COL_GATES = 0
COL_CQ = 3072
COL_CKV = 3328
COL_GQ = 3584
COL_GK = 3840
COL_GV = 4096
COL_GOG = 4608
COL_POOL = 5120
COL_SMALL = 5632
D_PROJ = 5760
SMALL_LR = MLA_ROPE

TM_PROJ = 1024
TN_PROJ = 1920
ATT_T = 512
ATT_NH = 4
V_ROWS = 80
GLA_L = 128
GLA_BASE = 1
GLA_NB = 4
TM_POOL = 256
POOL_HALO = 16
TM_MERGE = 512
TM_FFN = 1024
TF_FFN = 256
TM_ROUTE = 1024
TM_MOE = 512
TM_COMBINE = 512
GATHER_UNROLL = 8
VMEM_LIMIT = 56 * 1024 * 1024


def _cparams(*sem):
    return pltpu.CompilerParams(dimension_semantics=sem, vmem_limit_bytes=VMEM_LIMIT)


def _tile(n, pref):
    t = min(n, pref)
    assert n % t == 0, (n, pref)
    return t


def _rms(x):
    return x * lax.rsqrt(jnp.mean(x * x, axis=-1, keepdims=True) + EPS)


_NT = (((1,), (1,)), ((), ()))
_TN = (((0,), (0,)), ((), ()))


def _norm_matmul_body(x_ref, g_ref, w_ref, o_ref, h_ref):
    @pl.when(pl.program_id(1) == 0)
    def _():
        h_ref[...] = (_rms(x_ref[...]) * g_ref[...]).astype(BF16)

    o_ref[...] = jnp.dot(h_ref[...], w_ref[...], preferred_element_type=F32).astype(o_ref.dtype)


def _norm_matmul(x2d, g, w, out_dtype):
    t, d = x2d.shape
    n = w.shape[1]
    tm, tn = _tile(t, TM_PROJ), _tile(n, TN_PROJ)
    return pl.pallas_call(
        _norm_matmul_body,
        grid=(t // tm, n // tn),
        in_specs=[pl.BlockSpec((tm, d), lambda i, j: (i, 0)),
                  pl.BlockSpec((1, d), lambda i, j: (0, 0)),
                  pl.BlockSpec((d, tn), lambda i, j: (0, j))],
        out_specs=pl.BlockSpec((tm, tn), lambda i, j: (i, j)),
        out_shape=jax.ShapeDtypeStruct((t, n), out_dtype),
        scratch_shapes=[pltpu.VMEM((tm, d), BF16)],
        compiler_params=_cparams("parallel", "arbitrary"),
        name="norm_in_proj",
    )(x2d, g.reshape(1, d), w)


def _mla_prep_body(pos_ref, cq_ref, ckv_ref, small_ref, gcq_ref, wuq_ref, gckv_ref, wukv_ref, gq_ref, gk_ref,
                   invf_ref, qT_ref, k_ref, vT_ref):
    tp = cq_ref.shape[1]
    half = MLA_ROPE // 2
    ang = invf_ref[...] * pos_ref[0].astype(F32)
    cos, sin = jnp.cos(ang), jnp.sin(ang)

    def rope(t):
        x1, x2 = t[:half], t[half:]
        return x1 * cos - x2 * sin, x2 * cos + x1 * sin

    cqn = (_rms(cq_ref[0].astype(F32)) * gcq_ref[...]).astype(BF16)
    ckvn = (_rms(ckv_ref[0].astype(F32)) * gckv_ref[...]).astype(BF16)
    qT = lax.dot_general(wuq_ref[...], cqn, _NT, preferred_element_type=F32)
    kvT = lax.dot_general(wukv_ref[...], ckvn, _NT, preferred_element_type=F32)
    kr = small_ref[0].astype(F32).T[:MLA_ROPE]
    kr_ss = jnp.sum(kr * kr, axis=0, keepdims=True)
    gq, gk = gq_ref[...], gk_ref[...]
    zpad = jnp.zeros((HEAD_PAD - MLA_QK, tp), F32)
    vrow = lax.broadcasted_iota(jnp.int32, (V_ROWS - MLA_V, tp), 0)
    vpad = jnp.where(vrow == 0, 1.0, 0.0).astype(F32)
    scale = MLA_QK ** -0.5 * np.log2(np.e)
    for h in range(MLA_HEADS):
        qh = qT[h * MLA_QK:(h + 1) * MLA_QK]
        qn = qh * lax.rsqrt(jnp.mean(qh * qh, axis=0, keepdims=True) + EPS) * (gq * scale)
        r1, r2 = rope(qn[MLA_NOPE:])
        qT_ref[0, h] = jnp.concatenate([qn[:MLA_NOPE], r1, r2, zpad], axis=0).astype(BF16)
        base = h * (MLA_NOPE + MLA_V)
        kn = kvT[base:base + MLA_NOPE]
        r = lax.rsqrt((jnp.sum(kn * kn, axis=0, keepdims=True) + kr_ss) * (1.0 / MLA_QK) + EPS)
        r1, r2 = rope(kr * r * gk[MLA_NOPE:])
        kfm = jnp.concatenate([kn * r * gk[:MLA_NOPE], r1, r2, zpad], axis=0)
        k_ref[0, h] = kfm.T.astype(BF16)
        vT_ref[0, h, 0] = jnp.concatenate([kvT[base + MLA_NOPE:base + MLA_NOPE + MLA_V], vpad], axis=0).astype(BF16)


def _mla_prep(proj, positions, g_cq, w_uq, g_ckv, w_ukv, g_qk_q, g_qk_k):
    b, s, _ = proj.shape
    tp = _tile(s, ATT_T)
    ns = s // tp
    h = MLA_HEADS
    pos = positions.reshape(b * ns, 1, tp)
    inv_freq = (ROPE_BASE ** (-jnp.arange(MLA_ROPE // 2, dtype=F32) / (MLA_ROPE // 2))).reshape(-1, 1)
    const = lambda shape: pl.BlockSpec(shape, lambda bi, si: (0,) * len(shape))
    return pl.pallas_call(
        _mla_prep_body,
        grid=(b, ns),
        in_specs=[pl.BlockSpec((1, 1, tp), lambda bi, si: (bi * ns + si, 0, 0)),
                  pl.BlockSpec((1, tp, MLA_Q_RANK), lambda bi, si: (bi, si, COL_CQ // MLA_Q_RANK)),
                  pl.BlockSpec((1, tp, MLA_KV_RANK), lambda bi, si: (bi, si, COL_CKV // MLA_KV_RANK)),
                  pl.BlockSpec((1, tp, LANES), lambda bi, si: (bi, si, COL_SMALL // LANES)),
                  const((1, MLA_Q_RANK)), const((h * MLA_QK, MLA_Q_RANK)),
                  const((1, MLA_KV_RANK)), const((h * (MLA_NOPE + MLA_V), MLA_KV_RANK)),
                  const((MLA_QK, 1)), const((MLA_QK, 1)), const((MLA_ROPE // 2, 1))],
        out_specs=[pl.BlockSpec((1, h, HEAD_PAD, tp), lambda bi, si: (bi, 0, 0, si)),
                   pl.BlockSpec((1, h, tp, HEAD_PAD), lambda bi, si: (bi, 0, si, 0)),
                   pl.BlockSpec((1, h, 1, V_ROWS, tp), lambda bi, si: (bi, 0, si, 0, 0))],
        out_shape=[jax.ShapeDtypeStruct((b, h, HEAD_PAD, s), BF16),
                   jax.ShapeDtypeStruct((b, h, s, HEAD_PAD), BF16),
                   jax.ShapeDtypeStruct((b, h, ns, V_ROWS, tp), BF16)],
        compiler_params=_cparams("parallel", "parallel"),
        name="mla_prep",
    )(pos, proj, proj, proj, g_cq.reshape(1, -1), w_uq.T.astype(BF16), g_ckv.reshape(1, -1), w_ukv.T.astype(BF16),
      g_qk_q.reshape(-1, 1), g_qk_k.reshape(-1, 1), inv_freq)


def _attn_body(qT_ref, k_ref, vT_ref, o_ref, *, tq, nh):
    i = pl.program_id(2)

    def step(j, carry, masked):
        scores = []
        for h in range(nh):
            kb = k_ref[0, h, pl.ds(pl.multiple_of(j * tq, tq), tq), :]
            s = jnp.dot(kb, qT_ref[0, h], preferred_element_type=F32)
            if masked:
                kpos = lax.broadcasted_iota(jnp.int32, (tq, tq), 0)
                qpos = lax.broadcasted_iota(jnp.int32, (tq, tq), 1)
                s = jnp.where(kpos <= qpos, s, -jnp.inf)
            scores.append(s)
        out = []
        for h in range(nh):
            m, acc = carry[h]
            m_new = jnp.maximum(m, jnp.max(scores[h], axis=0, keepdims=True))
            p = jnp.exp2(scores[h] - m_new).astype(BF16)
            acc = jnp.exp2(m - m_new) * acc + jnp.dot(vT_ref[0, h, j], p, preferred_element_type=F32)
            out.append((m_new, acc))
        return tuple(out)

    carry = tuple((jnp.full((1, tq), -jnp.inf, F32), jnp.zeros((V_ROWS, tq), F32)) for _ in range(nh))
    carry = lax.fori_loop(0, i, functools.partial(step, masked=False), carry)
    carry = step(i, carry, True)
    for h in range(nh):
        acc = carry[h][1]
        o_ref[0, h] = (acc[:MLA_V] / acc[MLA_V:MLA_V + 1]).astype(o_ref.dtype)


def _attention(qT, k, vT):
    b, h, _, s = qT.shape
    tq = vT.shape[-1]
    nk = s // tq
    nh = ATT_NH
    return pl.pallas_call(
        functools.partial(_attn_body, tq=tq, nh=nh),
        grid=(b, h // nh, s // tq),
        in_specs=[pl.BlockSpec((1, nh, HEAD_PAD, tq), lambda bi, hi, qi: (bi, hi, 0, qi)),
                  pl.BlockSpec((1, nh, s, HEAD_PAD), lambda bi, hi, qi: (bi, hi, 0, 0)),
                  pl.BlockSpec((1, nh, nk, V_ROWS, tq), lambda bi, hi, qi: (bi, hi, 0, 0, 0))],
        out_specs=pl.BlockSpec((1, nh, MLA_V, tq), lambda bi, hi, qi: (bi, hi, 0, qi)),
        out_shape=jax.ShapeDtypeStruct((b, h, MLA_V, s), BF16),
        compiler_params=_cparams("parallel", "parallel", "arbitrary"),
        name="mla_attention",
    )(qT, k, vT)


def _gla_body(q_ref, k_ref, v_ref, og_ref, small_ref, wg_ref, bg_ref, gout_ref, o_ref, st_ref, *, blk):
    @pl.when(pl.program_id(1) == 0)
    def _():
        st_ref[...] = jnp.zeros_like(st_ref)

    hd = GLA_HEADS * GLA_DK
    nb = q_ref.shape[0]
    row = lax.broadcasted_iota(jnp.int32, (blk, blk), 0)
    col = lax.broadcasted_iota(jnp.int32, (blk, blk), 1)
    tril = (col <= row).astype(F32)

    def prepare(bb):
        lr = small_ref[bb][:, SMALL_LR:SMALL_LR + GLA_GATE_RANK].astype(F32)
        z = jnp.dot(lr, wg_ref[...], precision=HIGHEST, preferred_element_type=F32) + bg_ref[...]
        log_a = jax.nn.log_sigmoid(z) * (1.0 / GLA_TAU)
        bc = jnp.dot(tril, log_a, precision=HIGHEST, preferred_element_type=F32)
        q = q_ref[bb].astype(F32) * (GLA_DK ** -0.5)
        k = k_ref[bb].astype(F32)

        def ref_rows(group, off):
            parts = [jnp.broadcast_to(bc[g0 + off:g0 + off + 1], (group, hd)) for g0 in range(0, blk, group)]
            return parts[0] if len(parts) == 1 else jnp.concatenate(parts, axis=0)

        levels = []
        group = blk
        while group > GLA_BASE:
            half = group // 2
            ref = ref_rows(group, half - 1)
            qs = (q * jnp.exp(jnp.minimum(bc - ref, 0.0))).astype(BF16)
            ks = (k * jnp.exp(jnp.minimum(ref - bc, 0.0))).astype(BF16)
            mask = ((row & -group) == (col & -group)) & ((row & half) != 0) & ((col & half) == 0)
            levels.append((qs, ks, mask))
            group = half
        if GLA_BASE == 1:
            levels.append((q.astype(BF16), k.astype(BF16), row == col))
        else:
            ref = ref_rows(GLA_BASE, GLA_BASE // 2 - 1)
            mask = ((row & -GLA_BASE) == (col & -GLA_BASE)) & (col <= row)
            levels.append(((q * jnp.exp(bc - ref)).astype(BF16), (k * jnp.exp(ref - bc)).astype(BF16), mask))
        b_last = bc[blk - 1:blk]
        return dict(levels=levels, qd=(q * jnp.exp(bc)).astype(BF16), kd=(k * jnp.exp(b_last - bc)).astype(BF16),
                    a_last=jnp.exp(b_last))

    prep = [prepare(bb) for bb in range(nb)]
    gout = gout_ref[...]
    for h in range(GLA_HEADS):
        ks_, vs_ = slice(h * GLA_DK, (h + 1) * GLA_DK), slice(h * GLA_DV, (h + 1) * GLA_DV)
        for bb in range(nb):
            pr = prep[bb]
            attn = jnp.zeros((blk, blk), F32)
            for qs, ksc, mask in pr["levels"]:
                p = lax.dot_general(qs[:, ks_], ksc[:, ks_], _NT, preferred_element_type=F32)
                attn = jnp.where(mask, p, attn)
            v_h = v_ref[bb, :, vs_]
            st = st_ref[bb, h]
            o = jnp.dot(attn.astype(BF16), v_h, preferred_element_type=F32)
            o = o + lax.dot_general(pr["qd"][:, ks_], st.astype(BF16), _NT, preferred_element_type=F32)
            st_ref[bb, h] = (st * pr["a_last"][:, ks_]
                             + lax.dot_general(v_h, pr["kd"][:, ks_], _TN, preferred_element_type=F32))
            og = og_ref[bb, :, vs_].astype(F32)
            o_ref[bb, :, vs_] = (_rms(o) * gout * (og * jax.nn.sigmoid(og))).astype(o_ref.dtype)


def _gla(proj, w_gate_up, b_gate, g_out):
    b, s, _ = proj.shape
    blk = _tile(s, GLA_L)
    nb = _tile(b, GLA_NB)
    hd, hv = GLA_HEADS * GLA_DK, GLA_HEADS * GLA_DV
    const = lambda shape: pl.BlockSpec(shape, lambda bi, si: (0,) * len(shape))
    return pl.pallas_call(
        functools.partial(_gla_body, blk=blk),
        grid=(b // nb, s // blk),
        in_specs=[pl.BlockSpec((nb, blk, hd), lambda bi, si: (bi, si, COL_GQ // hd)),
                  pl.BlockSpec((nb, blk, hd), lambda bi, si: (bi, si, COL_GK // hd)),
                  pl.BlockSpec((nb, blk, hv), lambda bi, si: (bi, si, COL_GV // hv)),
                  pl.BlockSpec((nb, blk, hv), lambda bi, si: (bi, si, COL_GOG // hv)),
                  pl.BlockSpec((nb, blk, LANES), lambda bi, si: (bi, si, COL_SMALL // LANES)),
                  const((GLA_GATE_RANK, hd)), const((1, hd)), const((1, GLA_DV))],
        out_specs=pl.BlockSpec((nb, blk, hv), lambda bi, si: (bi, si, 0)),
        out_shape=jax.ShapeDtypeStruct((b, s, hv), BF16),
        scratch_shapes=[pltpu.VMEM((nb, GLA_HEADS, GLA_DV, GLA_DK), F32)],
        compiler_params=_cparams("parallel", "arbitrary"),
        name="gla",
    )(proj, proj, proj, proj, proj, w_gate_up, b_gate.reshape(1, -1), g_out.reshape(1, -1))


def _pool_body(u_ref, halo_ref, w_ref, scale_ref, o_ref, *, tm):
    si = pl.program_id(1)
    row = lax.broadcasted_iota(jnp.int32, (tm, tm + POOL_HALO), 0)
    col = lax.broadcasted_iota(jnp.int32, (tm, tm + POOL_HALO), 1)
    t1 = (si * tm + lax.broadcasted_iota(jnp.int32, (tm, 1), 0) + 1).astype(F32)
    for g, win in enumerate(POOL_WINDOWS):
        cs = slice(g * POOL_GROUP, (g + 1) * POOL_GROUP)
        cur = u_ref[0, :, cs]
        halo = halo_ref[0, :, cs]
        ext = jnp.concatenate([jnp.where(si > 0, halo, jnp.zeros_like(halo)), cur], axis=0)
        band = ((col <= row + POOL_HALO) & (col > row + POOL_HALO - win)).astype(BF16)
        wsum = jnp.dot(band, ext, preferred_element_type=F32)
        pooled = wsum / jnp.minimum(t1, float(win)) - cur.astype(F32)
        y = jnp.dot(pooled.astype(BF16), w_ref[g], preferred_element_type=F32)
        o_ref[0, :, cs] = (y * scale_ref[:, cs]).astype(o_ref.dtype)


def _pool(proj, w_pool, pool_scale):
    b, s, _ = proj.shape
    tm = _tile(s, TM_POOL)
    hb = tm // POOL_HALO
    return pl.pallas_call(
        functools.partial(_pool_body, tm=tm),
        grid=(b, s // tm),
        in_specs=[pl.BlockSpec((1, tm, POOL_WIDTH), lambda bi, si: (bi, si, COL_POOL // POOL_WIDTH)),
                  pl.BlockSpec((1, POOL_HALO, POOL_WIDTH),
                               lambda bi, si: (bi, jnp.maximum(si * hb - 1, 0), COL_POOL // POOL_WIDTH)),
                  pl.BlockSpec((len(POOL_WINDOWS), POOL_GROUP, POOL_GROUP), lambda bi, si: (0, 0, 0)),
                  pl.BlockSpec((1, POOL_WIDTH), lambda bi, si: (0, 0))],
        out_specs=pl.BlockSpec((1, tm, POOL_WIDTH), lambda bi, si: (bi, si, 0)),
        out_shape=jax.ShapeDtypeStruct((b, s, POOL_WIDTH), BF16),
        compiler_params=_cparams("parallel", "parallel"),
        name="pool",
    )(proj, proj, w_pool.astype(BF16), pool_scale.reshape(1, -1))


def _merge_body(x_ref, yaT_ref, yb_ref, yc_ref, g0_ref, g1_ref, g2_ref, wb_ref, wo_ref, o_ref):
    d_a = lax.dot_general(yaT_ref[0], wb_ref[0], _TN, preferred_element_type=F32)
    d_b = jnp.dot(yb_ref[0], wb_ref[1], preferred_element_type=F32)
    d_c = jnp.dot(yc_ref[0], wb_ref[2], preferred_element_type=F32)
    merged = (jax.nn.sigmoid(g0_ref[0].astype(F32)) * d_a + jax.nn.sigmoid(g1_ref[0].astype(F32)) * d_b
              + jax.nn.sigmoid(g2_ref[0].astype(F32)) * d_c)
    o_ref[0] = x_ref[0] + jnp.dot(merged.astype(BF16), wo_ref[...], preferred_element_type=F32)


def _merge(x, yaT, yb, yc, proj, w_branch, w_out):
    b, s, d = x.shape
    tm = _tile(s, TM_MERGE)
    tok = lambda width, cb: pl.BlockSpec((1, tm, width), lambda bi, si: (bi, si, cb))
    return pl.pallas_call(
        _merge_body,
        grid=(b, s // tm),
        in_specs=[tok(d, 0),
                  pl.BlockSpec((1, BRANCH_WIDTH, tm), lambda bi, si: (bi, 0, si)),
                  tok(BRANCH_WIDTH, 0), tok(BRANCH_WIDTH, 0),
                  tok(d, COL_GATES // d), tok(d, COL_GATES // d + 1), tok(d, COL_GATES // d + 2),
                  pl.BlockSpec((N_BRANCH, BRANCH_WIDTH, d), lambda bi, si: (0, 0, 0)),
                  pl.BlockSpec((d, d), lambda bi, si: (0, 0))],
        out_specs=tok(d, 0),
        out_shape=jax.ShapeDtypeStruct((b, s, d), F32),
        compiler_params=_cparams("parallel", "parallel"),
        name="merge_out_proj",
    )(x, yaT, yb, yc, proj, proj, proj, w_branch.astype(BF16), w_out.astype(BF16))


def _ffn_body(x_ref, g_ref, wg_ref, wu_ref, wd_ref, o_ref, h_ref, act_ref):
    f = pl.program_id(1)
    nf = act_ref.shape[0]

    @pl.when(f == 0)
    def _():
        h_ref[...] = (_rms(x_ref[...]) * g_ref[...]).astype(BF16)

    h = h_ref[...]
    a = jnp.dot(h, wg_ref[...], preferred_element_type=F32)
    u = jnp.dot(h, wu_ref[...], preferred_element_type=F32)
    act_ref[f] = (a * jax.nn.sigmoid(a) * u).astype(BF16)

    @pl.when(f == nf - 1)
    def _():
        act = jnp.concatenate([act_ref[c] for c in range(nf)], axis=1)
        o_ref[...] = x_ref[...] + jnp.dot(act, wd_ref[...], preferred_element_type=F32)


def _ffn(x2d, g, w_gate, w_up, w_down):
    t, d = x2d.shape
    ff = w_gate.shape[1]
    tm, tf = _tile(t, TM_FFN), _tile(ff, TF_FFN)
    return pl.pallas_call(
        _ffn_body,
        grid=(t // tm, ff // tf),
        in_specs=[pl.BlockSpec((tm, d), lambda i, f: (i, 0)),
                  pl.BlockSpec((1, d), lambda i, f: (0, 0)),
                  pl.BlockSpec((d, tf), lambda i, f: (0, f)),
                  pl.BlockSpec((d, tf), lambda i, f: (0, f)),
                  pl.BlockSpec((ff, d), lambda i, f: (0, 0))],
        out_specs=pl.BlockSpec((tm, d), lambda i, f: (i, 0)),
        out_shape=jax.ShapeDtypeStruct((t, d), F32),
        scratch_shapes=[pltpu.VMEM((tm, d), BF16), pltpu.VMEM((ff // tf, tm, tf), BF16)],
        compiler_params=_cparams("parallel", "arbitrary"),
        name="ffn_dense",
    )(x2d, g.reshape(1, d), w_gate.astype(BF16), w_up.astype(BF16), w_down.astype(BF16))


ROW_TILE = (8, LANES)


def _store_row_tiles(ref, val):
    for c in range(ROW_TILE[0]):
        ref[:, c, :] = val[:, c * LANES:(c + 1) * LANES]


def _load_row_tiles(ref_view):
    return jnp.concatenate([ref_view[:, c, :] for c in range(ROW_TILE[0])], axis=1)


def _route_body(x_ref, g_ref, wr_ref, h_ref, idx_ref, w_ref):
    hf = _rms(x_ref[...]) * g_ref[...]
    _store_row_tiles(h_ref, hf)
    logits = jnp.dot(hf, wr_ref[...], precision=HIGHEST, preferred_element_type=F32)
    lane = lax.broadcasted_iota(jnp.int32, logits.shape, 1)
    m1 = jnp.max(logits, axis=-1, keepdims=True)
    i1 = jnp.min(jnp.where(logits == m1, lane, N_EXPERTS), axis=-1, keepdims=True)
    rest = jnp.where(lane == i1, -jnp.inf, logits)
    m2 = jnp.max(rest, axis=-1, keepdims=True)
    i2 = jnp.min(jnp.where(rest == m2, lane, N_EXPERTS), axis=-1, keepdims=True)
    e2 = jnp.exp(m2 - m1)
    w1 = 1.0 / (1.0 + e2)
    first = lax.broadcasted_iota(jnp.int32, idx_ref.shape, 1) == 0
    idx_ref[...] = jnp.where(first, i1, i2)
    w_ref[...] = jnp.where(first, w1, e2 * w1)


def _moe_route(x2d, g, w_router):
    t, d = x2d.shape
    ne = w_router.shape[1]
    tm = _tile(t, TM_ROUTE)
    return pl.pallas_call(
        _route_body,
        grid=(t // tm,),
        in_specs=[pl.BlockSpec((tm, d), lambda i: (i, 0)),
                  pl.BlockSpec((1, d), lambda i: (0, 0)),
                  pl.BlockSpec((d, ne), lambda i: (0, 0))],
        out_specs=[pl.BlockSpec((tm,) + ROW_TILE, lambda i: (i, 0, 0)),
                   pl.BlockSpec((tm, TOP_K), lambda i: (i, 0)),
                   pl.BlockSpec((tm, TOP_K), lambda i: (i, 0))],
        out_shape=[jax.ShapeDtypeStruct((t,) + ROW_TILE, F32),
                   jax.ShapeDtypeStruct((t, TOP_K), jnp.int32),
                   jax.ShapeDtypeStruct((t, TOP_K), F32)],
        compiler_params=_cparams("parallel"),
        name="moe_route",
    )(x2d, g.reshape(1, d), w_router)


def _moe_plan(idx, tm):
    t = idx.shape[0]
    e = idx.reshape(-1)
    onehot = (e[:, None] == jnp.arange(N_EXPERTS, dtype=jnp.int32)[None, :]).astype(jnp.int32)
    csum = jnp.cumsum(onehot, axis=0)
    rank = jnp.sum((csum - onehot) * onehot, axis=1)
    tiles = (csum[-1] + tm - 1) // tm
    tile_end = jnp.cumsum(tiles)
    row0 = (tile_end - tiles) * tm
    dest = jnp.sum(onehot * row0[None, :], axis=1) + rank
    n_tiles = (TOP_K * t) // tm + N_EXPERTS
    row_token = jnp.zeros((n_tiles * tm,), jnp.int32).at[dest].set(
        jnp.arange(TOP_K * t, dtype=jnp.int32) // TOP_K, unique_indices=True)
    tile_expert = jnp.sum(jnp.arange(n_tiles, dtype=jnp.int32)[:, None] >= tile_end[None, :], axis=1)
    tile_expert = jnp.minimum(tile_expert, N_EXPERTS - 1).astype(jnp.int32)
    return row_token.reshape(n_tiles, 1, tm), dest.reshape(t, TOP_K), tile_expert, tile_end[-1:].astype(jnp.int32)


def _gather_rows(idx_ref, src_hbm, buf, sem, lo, n):
    def body(p, carry):
        rows = [lo + GATHER_UNROLL * p + q for q in range(GATHER_UNROLL)]
        toks = [idx_ref[0, 0, r] for r in rows]
        for q, (r, tok) in enumerate(zip(rows, toks)):
            pltpu.make_async_copy(src_hbm.at[pl.ds(tok, 1)], buf.at[pl.ds(r, 1)], sem).start(priority=q % 2)
        return carry

    lax.fori_loop(0, n // GATHER_UNROLL, body, 0)


def _wait_rows(src_hbm, buf, sem, n):
    pltpu.make_async_copy(src_hbm.at[pl.ds(0, n)], buf, sem).wait()


def _moe_ffn_body(te_ref, nu_ref, cur_ref, nxt_ref, h_hbm, wg_ref, wu_ref, wd_ref, o_ref, xbuf, sem,
                  wgb, wub, wdb, *, tm):
    i = pl.program_id(0)
    slot = i % 2
    n_used = nu_ref[0]

    @pl.when(jnp.logical_and(i == 0, n_used > 0))
    def _():
        _gather_rows(cur_ref, h_hbm, xbuf.at[0], sem.at[0], 0, tm)

    def prefetch(part):
        @pl.when(i + 1 < n_used)
        def _():
            _gather_rows(nxt_ref, h_hbm, xbuf.at[1 - slot], sem.at[1 - slot], part * (tm // 4), tm // 4)

    new_expert = jnp.logical_or(i == 0, te_ref[i] != te_ref[jnp.maximum(i - 1, 0)])

    @pl.when(jnp.logical_and(i < n_used, new_expert))
    def _():
        wgb[...] = wg_ref[0].astype(BF16)
        wub[...] = wu_ref[0].astype(BF16)
        wdb[...] = wd_ref[0].astype(BF16)

    @pl.when(i < n_used)
    def _():
        _wait_rows(h_hbm, xbuf.at[slot], sem.at[slot], tm)
        xs = _load_row_tiles(xbuf.at[slot]).astype(BF16)
        prefetch(0)
        a = jnp.dot(xs, wgb[...], preferred_element_type=F32)
        prefetch(1)
        u = jnp.dot(xs, wub[...], preferred_element_type=F32)
        prefetch(2)
        act = (a * jax.nn.sigmoid(a) * u).astype(BF16)
        y = jnp.dot(act, wdb[...], preferred_element_type=F32)
        prefetch(3)
        _store_row_tiles(o_ref, y)

    @pl.when(i >= n_used)
    def _():
        o_ref[...] = jnp.zeros_like(o_ref)


def _moe_ffn(h, row_token, tile_expert, n_used, w_gate, w_up, w_down):
    ne, d, fe = w_gate.shape
    assert h.shape[1:] == ROW_TILE and d == ROW_TILE[0] * ROW_TILE[1]
    n_tiles, _, tm = row_token.shape
    grid_spec = pltpu.PrefetchScalarGridSpec(
        num_scalar_prefetch=2,
        grid=(n_tiles,),
        in_specs=[pl.BlockSpec((1, 1, tm), lambda i, te, nu: (i, 0, 0), memory_space=pltpu.SMEM),
                  pl.BlockSpec((1, 1, tm), lambda i, te, nu: (jnp.minimum(i + 1, n_tiles - 1), 0, 0),
                               memory_space=pltpu.SMEM),
                  pl.BlockSpec(memory_space=pl.ANY),
                  pl.BlockSpec((1, d, fe), lambda i, te, nu: (te[i], 0, 0), pipeline_mode=pl.Buffered(1)),
                  pl.BlockSpec((1, d, fe), lambda i, te, nu: (te[i], 0, 0), pipeline_mode=pl.Buffered(1)),
                  pl.BlockSpec((1, fe, d), lambda i, te, nu: (te[i], 0, 0), pipeline_mode=pl.Buffered(1))],
        out_specs=pl.BlockSpec((tm,) + ROW_TILE, lambda i, te, nu: (i, 0, 0)),
        scratch_shapes=[pltpu.VMEM((2, tm) + ROW_TILE, F32), pltpu.SemaphoreType.DMA((2,)),
                        pltpu.VMEM((d, fe), BF16), pltpu.VMEM((d, fe), BF16), pltpu.VMEM((fe, d), BF16)])
    return pl.pallas_call(
        functools.partial(_moe_ffn_body, tm=tm),
        grid_spec=grid_spec,
        out_shape=jax.ShapeDtypeStruct((n_tiles * tm,) + ROW_TILE, F32),
        compiler_params=_cparams("arbitrary"),
        name="moe_ffn",
    )(tile_expert, n_used, row_token, row_token, h, w_gate, w_up, w_down)


def _moe_combine_body(cur_ref, nxt_ref, x_ref, w_ref, y_hbm, o_ref, ybuf, sem, *, tc):
    i = pl.program_id(0)
    slot = i % 2
    n = TOP_K * tc

    @pl.when(i == 0)
    def _():
        _gather_rows(cur_ref, y_hbm, ybuf.at[0], sem.at[0], 0, n)

    @pl.when(i + 1 < pl.num_programs(0))
    def _():
        _gather_rows(nxt_ref, y_hbm, ybuf.at[1 - slot], sem.at[1 - slot], 0, n)

    _wait_rows(y_hbm, ybuf.at[slot], sem.at[slot], n)
    w = w_ref[...]
    y0 = _load_row_tiles(ybuf.at[slot, pl.ds(0, tc)])
    y1 = _load_row_tiles(ybuf.at[slot, pl.ds(tc, tc)])
    o_ref[...] = x_ref[...] + w[:, 0:1] * y0 + w[:, 1:2] * y1


def _moe_combine(x2d, ys, dest, w):
    t, d = x2d.shape
    tc = _tile(t, TM_COMBINE)
    nt = t // tc
    dest_tiles = dest.reshape(nt, tc, TOP_K).transpose(0, 2, 1).reshape(nt, 1, TOP_K * tc)
    return pl.pallas_call(
        functools.partial(_moe_combine_body, tc=tc),
        grid=(nt,),
        in_specs=[pl.BlockSpec((1, 1, TOP_K * tc), lambda i: (i, 0, 0), memory_space=pltpu.SMEM),
                  pl.BlockSpec((1, 1, TOP_K * tc), lambda i: (jnp.minimum(i + 1, nt - 1), 0, 0),
                               memory_space=pltpu.SMEM),
                  pl.BlockSpec((tc, d), lambda i: (i, 0)),
                  pl.BlockSpec((tc, TOP_K), lambda i: (i, 0)),
                  pl.BlockSpec(memory_space=pl.ANY)],
        out_specs=pl.BlockSpec((tc, d), lambda i: (i, 0)),
        out_shape=jax.ShapeDtypeStruct((t, d), F32),
        scratch_shapes=[pltpu.VMEM((2, TOP_K * tc) + ROW_TILE, F32), pltpu.SemaphoreType.DMA((2,))],
        compiler_params=_cparams("arbitrary"),
        name="moe_combine",
    )(dest_tiles, dest_tiles, x2d, w, ys)


def _moe(x2d, g, w_router, w_gate, w_up, w_down):
    h, idx, w = _moe_route(x2d, g, w_router)
    row_token, dest, tile_expert, n_used = _moe_plan(idx, _tile(TOP_K * x2d.shape[0], TM_MOE))
    ys = _moe_ffn(h, row_token, tile_expert, n_used, w_gate, w_up, w_down)
    return _moe_combine(x2d, ys, dest, w)


def _permute_w_in(w):
    off = np.concatenate([[0], np.cumsum(IN_SPLITS)]).tolist()
    piece = lambda i: w[:, off[i]:off[i + 1]].astype(BF16)
    order = [9, 0, 1, 3, 4, 5, 7, 8, 2, 6]
    pad = jnp.zeros((w.shape[0], D_PROJ - off[-1]), BF16)
    return jnp.concatenate([piece(i) for i in order] + [pad], axis=1)


def kernel(x, positions, g_mix, w_in, g_cq, w_uq, g_ckv, w_ukv, g_qk_q, g_qk_k, w_gla_gate_up, b_gla_gate, g_gla_out, w_pool, pool_scale, w_branch, w_out, g_ffn, w_ffn_gate, w_ffn_up, w_ffn_down, w_router, w_exp_gate, w_exp_up, w_exp_down):
    b, s, d = x.shape
    depth = g_mix.shape[0]
    for layer in range(depth):
        proj = _norm_matmul(x.reshape(b * s, d), g_mix[layer], _permute_w_in(w_in[layer]), BF16).reshape(b, s, D_PROJ)
        qT, k, vT = _mla_prep(proj, positions, g_cq[layer], w_uq[layer], g_ckv[layer], w_ukv[layer],
                              g_qk_q[layer], g_qk_k[layer])
        yaT = _attention(qT, k, vT).reshape(b, MLA_HEADS * MLA_V, s)
        yb = _gla(proj, w_gla_gate_up[layer], b_gla_gate[layer], g_gla_out[layer])
        yc = _pool(proj, w_pool[layer], pool_scale[layer])
        x = _merge(x, yaT, yb, yc, proj, w_branch[layer], w_out[layer])
        i = layer // 2
        if layer % 2 == 0:
            x2d = _ffn(x.reshape(b * s, d), g_ffn[layer], w_ffn_gate[i], w_ffn_up[i], w_ffn_down[i])
        else:
            x2d = _moe(x.reshape(b * s, d), g_ffn[layer], w_router[i], w_exp_gate[i], w_exp_up[i], w_exp_down[i])
        x = x2d.reshape(b, s, d)
    return x
```

```python
import functools

import jax
import jax.numpy as jnp
import numpy as np
from jax import lax
from jax.experimental import pallas as pl
from jax.experimental.pallas import tpu as pltpu

F32 = jnp.float32
BF16 = jnp.bfloat16
HIGHEST = lax.Precision.HIGHEST

D_MODEL = 1024
MLA_HEADS = 8
MLA_NOPE = 64
MLA_ROPE = 32
MLA_QK = MLA_NOPE + MLA_ROPE
MLA_V = 64
MLA_Q_RANK = D_MODEL // 4
MLA_KV_RANK = D_MODEL // 4
ROPE_BASE = 10000.0
GLA_HEADS = 4
GLA_DK = 64
GLA_DV = 128
GLA_GATE_RANK = 16
GLA_TAU = 16.0
POOL_WINDOWS = (2, 4, 8, 16)
POOL_GROUP = 128
POOL_WIDTH = 4 * POOL_GROUP
N_BRANCH = 3
BRANCH_WIDTH = 512
D_FF = 2816
N_EXPERTS = 8
TOP_K = 2
D_EXPERT = 1408
EPS = 1e-6

IN_SPLITS = (MLA_Q_RANK, MLA_KV_RANK, MLA_ROPE,
             GLA_HEADS * GLA_DK, GLA_HEADS * GLA_DK, GLA_HEADS * GLA_DV, GLA_GATE_RANK, GLA_HEADS * GLA_DV,
             POOL_WIDTH, N_BRANCH * D_MODEL)

LANES = 128
HEAD_PAD = LANES

COL_GATES = 0
COL_CQ = 3072
COL_CKV = 3328
COL_GQ = 3584
COL_GK = 3840
COL_GV = 4096
COL_GOG = 4608
COL_POOL = 5120
COL_SMALL = 5632
D_PROJ = 5760
SMALL_LR = MLA_ROPE

TM_PROJ = 1024
TN_PROJ = 1920
ATT_T = 512
ATT_NH = 4
V_ROWS = 80
GLA_L = 128
GLA_BASE = 1
GLA_NB = 4
TM_POOL = 256
POOL_HALO = 16
TM_MERGE = 512
TM_FFN = 1024
TF_FFN = 256
TM_ROUTE = 1024
TM_MOE = 512
TM_COMBINE = 512
GATHER_UNROLL = 8
VMEM_LIMIT = 56 * 1024 * 1024


def _cparams(*sem):
    return pltpu.CompilerParams(dimension_semantics=sem, vmem_limit_bytes=VMEM_LIMIT)


def _tile(n, pref):
    t = min(n, pref)
    assert n % t == 0, (n, pref)
    return t


def _rms(x):
    return x * lax.rsqrt(jnp.mean(x * x, axis=-1, keepdims=True) + EPS)


_NT = (((1,), (1,)), ((), ()))
_TN = (((0,), (0,)), ((), ()))


def _norm_matmul_body(x_ref, g_ref, w_ref, o_ref, h_ref):
    @pl.when(pl.program_id(1) == 0)
    def _():
        h_ref[...] = (_rms(x_ref[...]) * g_ref[...]).astype(BF16)

    o_ref[...] = jnp.dot(h_ref[...], w_ref[...], preferred_element_type=F32).astype(o_ref.dtype)


def _norm_matmul(x2d, g, w, out_dtype):
    t, d = x2d.shape
    n = w.shape[1]
    tm, tn = _tile(t, TM_PROJ), _tile(n, TN_PROJ)
    return pl.pallas_call(
        _norm_matmul_body,
        grid=(t // tm, n // tn),
        in_specs=[pl.BlockSpec((tm, d), lambda i, j: (i, 0)),
                  pl.BlockSpec((1, d), lambda i, j: (0, 0)),
                  pl.BlockSpec((d, tn), lambda i, j: (0, j))],
        out_specs=pl.BlockSpec((tm, tn), lambda i, j: (i, j)),
        out_shape=jax.ShapeDtypeStruct((t, n), out_dtype),
        scratch_shapes=[pltpu.VMEM((tm, d), BF16)],
        compiler_params=_cparams("parallel", "arbitrary"),
        name="norm_in_proj",
    )(x2d, g.reshape(1, d), w)


def _mla_prep_body(pos_ref, cq_ref, ckv_ref, small_ref, gcq_ref, wuq_ref, gckv_ref, wukv_ref, gq_ref, gk_ref,
                   invf_ref, qT_ref, k_ref, vT_ref):
    tp = cq_ref.shape[1]
    half = MLA_ROPE // 2
    ang = invf_ref[...] * pos_ref[0].astype(F32)
    cos, sin = jnp.cos(ang), jnp.sin(ang)

    def rope(t):
        x1, x2 = t[:half], t[half:]
        return x1 * cos - x2 * sin, x2 * cos + x1 * sin

    cqn = (_rms(cq_ref[0].astype(F32)) * gcq_ref[...]).astype(BF16)
    ckvn = (_rms(ckv_ref[0].astype(F32)) * gckv_ref[...]).astype(BF16)
    qT = lax.dot_general(wuq_ref[...], cqn, _NT, preferred_element_type=F32)
    kvT = lax.dot_general(wukv_ref[...], ckvn, _NT, preferred_element_type=F32)
    kr = small_ref[0].astype(F32).T[:MLA_ROPE]
    kr_ss = jnp.sum(kr * kr, axis=0, keepdims=True)
    gq, gk = gq_ref[...], gk_ref[...]
    zpad = jnp.zeros((HEAD_PAD - MLA_QK, tp), F32)
    vrow = lax.broadcasted_iota(jnp.int32, (V_ROWS - MLA_V, tp), 0)
    vpad = jnp.where(vrow == 0, 1.0, 0.0).astype(F32)
    scale = MLA_QK ** -0.5 * np.log2(np.e)
    for h in range(MLA_HEADS):
        qh = qT[h * MLA_QK:(h + 1) * MLA_QK]
        qn = qh * lax.rsqrt(jnp.mean(qh * qh, axis=0, keepdims=True) + EPS) * (gq * scale)
        r1, r2 = rope(qn[MLA_NOPE:])
        qT_ref[0, h] = jnp.concatenate([qn[:MLA_NOPE], r1, r2, zpad], axis=0).astype(BF16)
        base = h * (MLA_NOPE + MLA_V)
        kn = kvT[base:base + MLA_NOPE]
        r = lax.rsqrt((jnp.sum(kn * kn, axis=0, keepdims=True) + kr_ss) * (1.0 / MLA_QK) + EPS)
        r1, r2 = rope(kr * r * gk[MLA_NOPE:])
        kfm = jnp.concatenate([kn * r * gk[:MLA_NOPE], r1, r2, zpad], axis=0)
        k_ref[0, h] = kfm.T.astype(BF16)
        vT_ref[0, h, 0] = jnp.concatenate([kvT[base + MLA_NOPE:base + MLA_NOPE + MLA_V], vpad], axis=0).astype(BF16)


def _mla_prep(proj, positions, g_cq, w_uq, g_ckv, w_ukv, g_qk_q, g_qk_k):
    b, s, _ = proj.shape
    tp = _tile(s, ATT_T)
    ns = s // tp
    h = MLA_HEADS
    pos = positions.reshape(b * ns, 1, tp)
    inv_freq = (ROPE_BASE ** (-jnp.arange(MLA_ROPE // 2, dtype=F32) / (MLA_ROPE // 2))).reshape(-1, 1)
    const = lambda shape: pl.BlockSpec(shape, lambda bi, si: (0,) * len(shape))
    return pl.pallas_call(
        _mla_prep_body,
        grid=(b, ns),
        in_specs=[pl.BlockSpec((1, 1, tp), lambda bi, si: (bi * ns + si, 0, 0)),
                  pl.BlockSpec((1, tp, MLA_Q_RANK), lambda bi, si: (bi, si, COL_CQ // MLA_Q_RANK)),
                  pl.BlockSpec((1, tp, MLA_KV_RANK), lambda bi, si: (bi, si, COL_CKV // MLA_KV_RANK)),
                  pl.BlockSpec((1, tp, LANES), lambda bi, si: (bi, si, COL_SMALL // LANES)),
                  const((1, MLA_Q_RANK)), const((h * MLA_QK, MLA_Q_RANK)),
                  const((1, MLA_KV_RANK)), const((h * (MLA_NOPE + MLA_V), MLA_KV_RANK)),
                  const((MLA_QK, 1)), const((MLA_QK, 1)), const((MLA_ROPE // 2, 1))],
        out_specs=[pl.BlockSpec((1, h, HEAD_PAD, tp), lambda bi, si: (bi, 0, 0, si)),
                   pl.BlockSpec((1, h, tp, HEAD_PAD), lambda bi, si: (bi, 0, si, 0)),
                   pl.BlockSpec((1, h, 1, V_ROWS, tp), lambda bi, si: (bi, 0, si, 0, 0))],
        out_shape=[jax.ShapeDtypeStruct((b, h, HEAD_PAD, s), BF16),
                   jax.ShapeDtypeStruct((b, h, s, HEAD_PAD), BF16),
                   jax.ShapeDtypeStruct((b, h, ns, V_ROWS, tp), BF16)],
        compiler_params=_cparams("parallel", "parallel"),
        name="mla_prep",
    )(pos, proj, proj, proj, g_cq.reshape(1, -1), w_uq.T.astype(BF16), g_ckv.reshape(1, -1), w_ukv.T.astype(BF16),
      g_qk_q.reshape(-1, 1), g_qk_k.reshape(-1, 1), inv_freq)


def _attn_body(qT_ref, k_ref, vT_ref, o_ref, *, tq, nh):
    i = pl.program_id(2)

    def step(j, carry, masked):
        scores = []
        for h in range(nh):
            kb = k_ref[0, h, pl.ds(pl.multiple_of(j * tq, tq), tq), :]
            s = jnp.dot(kb, qT_ref[0, h], preferred_element_type=F32)
            if masked:
                kpos = lax.broadcasted_iota(jnp.int32, (tq, tq), 0)
                qpos = lax.broadcasted_iota(jnp.int32, (tq, tq), 1)
                s = jnp.where(kpos <= qpos, s, -jnp.inf)
            scores.append(s)
        out = []
        for h in range(nh):
            m, acc = carry[h]
            m_new = jnp.maximum(m, jnp.max(scores[h], axis=0, keepdims=True))
            p = jnp.exp2(scores[h] - m_new).astype(BF16)
            acc = jnp.exp2(m - m_new) * acc + jnp.dot(vT_ref[0, h, j], p, preferred_element_type=F32)
            out.append((m_new, acc))
        return tuple(out)

    carry = tuple((jnp.full((1, tq), -jnp.inf, F32), jnp.zeros((V_ROWS, tq), F32)) for _ in range(nh))
    carry = lax.fori_loop(0, i, functools.partial(step, masked=False), carry)
    carry = step(i, carry, True)
    for h in range(nh):
        acc = carry[h][1]
        o_ref[0, h] = (acc[:MLA_V] / acc[MLA_V:MLA_V + 1]).astype(o_ref.dtype)


def _attention(qT, k, vT):
    b, h, _, s = qT.shape
    tq = vT.shape[-1]
    nk = s // tq
    nh = ATT_NH
    return pl.pallas_call(
        functools.partial(_attn_body, tq=tq, nh=nh),
        grid=(b, h // nh, s // tq),
        in_specs=[pl.BlockSpec((1, nh, HEAD_PAD, tq), lambda bi, hi, qi: (bi, hi, 0, qi)),
                  pl.BlockSpec((1, nh, s, HEAD_PAD), lambda bi, hi, qi: (bi, hi, 0, 0)),
                  pl.BlockSpec((1, nh, nk, V_ROWS, tq), lambda bi, hi, qi: (bi, hi, 0, 0, 0))],
        out_specs=pl.BlockSpec((1, nh, MLA_V, tq), lambda bi, hi, qi: (bi, hi, 0, qi)),
        out_shape=jax.ShapeDtypeStruct((b, h, MLA_V, s), BF16),
        compiler_params=_cparams("parallel", "parallel", "arbitrary"),
        name="mla_attention",
    )(qT, k, vT)


def _gla_body(q_ref, k_ref, v_ref, og_ref, small_ref, wg_ref, bg_ref, gout_ref, o_ref, st_ref, *, blk):
    @pl.when(pl.program_id(1) == 0)
    def _():
        st_ref[...] = jnp.zeros_like(st_ref)

    hd = GLA_HEADS * GLA_DK
    nb = q_ref.shape[0]
    row = lax.broadcasted_iota(jnp.int32, (blk, blk), 0)
    col = lax.broadcasted_iota(jnp.int32, (blk, blk), 1)
    tril = (col <= row).astype(F32)

    def prepare(bb):
        lr = small_ref[bb][:, SMALL_LR:SMALL_LR + GLA_GATE_RANK].astype(F32)
        z = jnp.dot(lr, wg_ref[...], precision=HIGHEST, preferred_element_type=F32) + bg_ref[...]
        log_a = jax.nn.log_sigmoid(z) * (1.0 / GLA_TAU)
        bc = jnp.dot(tril, log_a, precision=HIGHEST, preferred_element_type=F32)
        q = q_ref[bb].astype(F32) * (GLA_DK ** -0.5)
        k = k_ref[bb].astype(F32)

        def ref_rows(group, off):
            parts = [jnp.broadcast_to(bc[g0 + off:g0 + off + 1], (group, hd)) for g0 in range(0, blk, group)]
            return parts[0] if len(parts) == 1 else jnp.concatenate(parts, axis=0)

        levels = []
        group = blk
        while group > GLA_BASE:
            half = group // 2
            ref = ref_rows(group, half - 1)
            qs = (q * jnp.exp(jnp.minimum(bc - ref, 0.0))).astype(BF16)
            ks = (k * jnp.exp(jnp.minimum(ref - bc, 0.0))).astype(BF16)
            mask = ((row & -group) == (col & -group)) & ((row & half) != 0) & ((col & half) == 0)
            levels.append((qs, ks, mask))
            group = half
        if GLA_BASE == 1:
            levels.append((q.astype(BF16), k.astype(BF16), row == col))
        else:
            ref = ref_rows(GLA_BASE, GLA_BASE // 2 - 1)
            mask = ((row & -GLA_BASE) == (col & -GLA_BASE)) & (col <= row)
            levels.append(((q * jnp.exp(bc - ref)).astype(BF16), (k * jnp.exp(ref - bc)).astype(BF16), mask))
        b_last = bc[blk - 1:blk]
        return dict(levels=levels, qd=(q * jnp.exp(bc)).astype(BF16), kd=(k * jnp.exp(b_last - bc)).astype(BF16),
                    a_last=jnp.exp(b_last))

    prep = [prepare(bb) for bb in range(nb)]
    gout = gout_ref[...]
    for h in range(GLA_HEADS):
        ks_, vs_ = slice(h * GLA_DK, (h + 1) * GLA_DK), slice(h * GLA_DV, (h + 1) * GLA_DV)
        for bb in range(nb):
            pr = prep[bb]
            attn = jnp.zeros((blk, blk), F32)
            for qs, ksc, mask in pr["levels"]:
                p = lax.dot_general(qs[:, ks_], ksc[:, ks_], _NT, preferred_element_type=F32)
                attn = jnp.where(mask, p, attn)
            v_h = v_ref[bb, :, vs_]
            st = st_ref[bb, h]
            o = jnp.dot(attn.astype(BF16), v_h, preferred_element_type=F32)
            o = o + lax.dot_general(pr["qd"][:, ks_], st.astype(BF16), _NT, preferred_element_type=F32)
            st_ref[bb, h] = (st * pr["a_last"][:, ks_]
                             + lax.dot_general(v_h, pr["kd"][:, ks_], _TN, preferred_element_type=F32))
            og = og_ref[bb, :, vs_].astype(F32)
            o_ref[bb, :, vs_] = (_rms(o) * gout * (og * jax.nn.sigmoid(og))).astype(o_ref.dtype)


def _gla(proj, w_gate_up, b_gate, g_out):
    b, s, _ = proj.shape
    blk = _tile(s, GLA_L)
    nb = _tile(b, GLA_NB)
    hd, hv = GLA_HEADS * GLA_DK, GLA_HEADS * GLA_DV
    const = lambda shape: pl.BlockSpec(shape, lambda bi, si: (0,) * len(shape))
    return pl.pallas_call(
        functools.partial(_gla_body, blk=blk),
        grid=(b // nb, s // blk),
        in_specs=[pl.BlockSpec((nb, blk, hd), lambda bi, si: (bi, si, COL_GQ // hd)),
                  pl.BlockSpec((nb, blk, hd), lambda bi, si: (bi, si, COL_GK // hd)),
                  pl.BlockSpec((nb, blk, hv), lambda bi, si: (bi, si, COL_GV // hv)),
                  pl.BlockSpec((nb, blk, hv), lambda bi, si: (bi, si, COL_GOG // hv)),
                  pl.BlockSpec((nb, blk, LANES), lambda bi, si: (bi, si, COL_SMALL // LANES)),
                  const((GLA_GATE_RANK, hd)), const((1, hd)), const((1, GLA_DV))],
        out_specs=pl.BlockSpec((nb, blk, hv), lambda bi, si: (bi, si, 0)),
        out_shape=jax.ShapeDtypeStruct((b, s, hv), BF16),
        scratch_shapes=[pltpu.VMEM((nb, GLA_HEADS, GLA_DV, GLA_DK), F32)],
        compiler_params=_cparams("parallel", "arbitrary"),
        name="gla",
    )(proj, proj, proj, proj, proj, w_gate_up, b_gate.reshape(1, -1), g_out.reshape(1, -1))


def _pool_body(u_ref, halo_ref, w_ref, scale_ref, o_ref, *, tm):
    si = pl.program_id(1)
    row = lax.broadcasted_iota(jnp.int32, (tm, tm + POOL_HALO), 0)
    col = lax.broadcasted_iota(jnp.int32, (tm, tm + POOL_HALO), 1)
    t1 = (si * tm + lax.broadcasted_iota(jnp.int32, (tm, 1), 0) + 1).astype(F32)
    for g, win in enumerate(POOL_WINDOWS):
        cs = slice(g * POOL_GROUP, (g + 1) * POOL_GROUP)
        cur = u_ref[0, :, cs]
        halo = halo_ref[0, :, cs]
        ext = jnp.concatenate([jnp.where(si > 0, halo, jnp.zeros_like(halo)), cur], axis=0)
        band = ((col <= row + POOL_HALO) & (col > row + POOL_HALO - win)).astype(BF16)
        wsum = jnp.dot(band, ext, preferred_element_type=F32)
        pooled = wsum / jnp.minimum(t1, float(win)) - cur.astype(F32)
        y = jnp.dot(pooled.astype(BF16), w_ref[g], preferred_element_type=F32)
        o_ref[0, :, cs] = (y * scale_ref[:, cs]).astype(o_ref.dtype)


def _pool(proj, w_pool, pool_scale):
    b, s, _ = proj.shape
    tm = _tile(s, TM_POOL)
    hb = tm // POOL_HALO
    return pl.pallas_call(
        functools.partial(_pool_body, tm=tm),
        grid=(b, s // tm),
        in_specs=[pl.BlockSpec((1, tm, POOL_WIDTH), lambda bi, si: (bi, si, COL_POOL // POOL_WIDTH)),
                  pl.BlockSpec((1, POOL_HALO, POOL_WIDTH),
                               lambda bi, si: (bi, jnp.maximum(si * hb - 1, 0), COL_POOL // POOL_WIDTH)),
                  pl.BlockSpec((len(POOL_WINDOWS), POOL_GROUP, POOL_GROUP), lambda bi, si: (0, 0, 0)),
                  pl.BlockSpec((1, POOL_WIDTH), lambda bi, si: (0, 0))],
        out_specs=pl.BlockSpec((1, tm, POOL_WIDTH), lambda bi, si: (bi, si, 0)),
        out_shape=jax.ShapeDtypeStruct((b, s, POOL_WIDTH), BF16),
        compiler_params=_cparams("parallel", "parallel"),
        name="pool",
    )(proj, proj, w_pool.astype(BF16), pool_scale.reshape(1, -1))


def _merge_body(x_ref, yaT_ref, yb_ref, yc_ref, g0_ref, g1_ref, g2_ref, wb_ref, wo_ref, o_ref):
    d_a = lax.dot_general(yaT_ref[0], wb_ref[0], _TN, preferred_element_type=F32)
    d_b = jnp.dot(yb_ref[0], wb_ref[1], preferred_element_type=F32)
    d_c = jnp.dot(yc_ref[0], wb_ref[2], preferred_element_type=F32)
    merged = (jax.nn.sigmoid(g0_ref[0].astype(F32)) * d_a + jax.nn.sigmoid(g1_ref[0].astype(F32)) * d_b
              + jax.nn.sigmoid(g2_ref[0].astype(F32)) * d_c)
    o_ref[0] = x_ref[0] + jnp.dot(merged.astype(BF16), wo_ref[...], preferred_element_type=F32)


def _merge(x, yaT, yb, yc, proj, w_branch, w_out):
    b, s, d = x.shape
    tm = _tile(s, TM_MERGE)
    tok = lambda width, cb: pl.BlockSpec((1, tm, width), lambda bi, si: (bi, si, cb))
    return pl.pallas_call(
        _merge_body,
        grid=(b, s // tm),
        in_specs=[tok(d, 0),
                  pl.BlockSpec((1, BRANCH_WIDTH, tm), lambda bi, si: (bi, 0, si)),
                  tok(BRANCH_WIDTH, 0), tok(BRANCH_WIDTH, 0),
                  tok(d, COL_GATES // d), tok(d, COL_GATES // d + 1), tok(d, COL_GATES // d + 2),
                  pl.BlockSpec((N_BRANCH, BRANCH_WIDTH, d), lambda bi, si: (0, 0, 0)),
                  pl.BlockSpec((d, d), lambda bi, si: (0, 0))],
        out_specs=tok(d, 0),
        out_shape=jax.ShapeDtypeStruct((b, s, d), F32),
        compiler_params=_cparams("parallel", "parallel"),
        name="merge_out_proj",
    )(x, yaT, yb, yc, proj, proj, proj, w_branch.astype(BF16), w_out.astype(BF16))


def _ffn_body(x_ref, g_ref, wg_ref, wu_ref, wd_ref, o_ref, h_ref, act_ref):
    f = pl.program_id(1)
    nf = act_ref.shape[0]

    @pl.when(f == 0)
    def _():
        h_ref[...] = (_rms(x_ref[...]) * g_ref[...]).astype(BF16)

    h = h_ref[...]
    a = jnp.dot(h, wg_ref[...], preferred_element_type=F32)
    u = jnp.dot(h, wu_ref[...], preferred_element_type=F32)
    act_ref[f] = (a * jax.nn.sigmoid(a) * u).astype(BF16)

    @pl.when(f == nf - 1)
    def _():
        act = jnp.concatenate([act_ref[c] for c in range(nf)], axis=1)
        o_ref[...] = x_ref[...] + jnp.dot(act, wd_ref[...], preferred_element_type=F32)


def _ffn(x2d, g, w_gate, w_up, w_down):
    t, d = x2d.shape
    ff = w_gate.shape[1]
    tm, tf = _tile(t, TM_FFN), _tile(ff, TF_FFN)
    return pl.pallas_call(
        _ffn_body,
        grid=(t // tm, ff // tf),
        in_specs=[pl.BlockSpec((tm, d), lambda i, f: (i, 0)),
                  pl.BlockSpec((1, d), lambda i, f: (0, 0)),
                  pl.BlockSpec((d, tf), lambda i, f: (0, f)),
                  pl.BlockSpec((d, tf), lambda i, f: (0, f)),
                  pl.BlockSpec((ff, d), lambda i, f: (0, 0))],
        out_specs=pl.BlockSpec((tm, d), lambda i, f: (i, 0)),
        out_shape=jax.ShapeDtypeStruct((t, d), F32),
        scratch_shapes=[pltpu.VMEM((tm, d), BF16), pltpu.VMEM((ff // tf, tm, tf), BF16)],
        compiler_params=_cparams("parallel", "arbitrary"),
        name="ffn_dense",
    )(x2d, g.reshape(1, d), w_gate.astype(BF16), w_up.astype(BF16), w_down.astype(BF16))


ROW_TILE = (8, LANES)


def _store_row_tiles(ref, val):
    for c in range(ROW_TILE[0]):
        ref[:, c, :] = val[:, c * LANES:(c + 1) * LANES]


def _load_row_tiles(ref_view):
    return jnp.concatenate([ref_view[:, c, :] for c in range(ROW_TILE[0])], axis=1)


def _route_body(x_ref, g_ref, wr_ref, h_ref, idx_ref, w_ref):
    hf = _rms(x_ref[...]) * g_ref[...]
    _store_row_tiles(h_ref, hf)
    logits = jnp.dot(hf, wr_ref[...], precision=HIGHEST, preferred_element_type=F32)
    lane = lax.broadcasted_iota(jnp.int32, logits.shape, 1)
    m1 = jnp.max(logits, axis=-1, keepdims=True)
    i1 = jnp.min(jnp.where(logits == m1, lane, N_EXPERTS), axis=-1, keepdims=True)
    rest = jnp.where(lane == i1, -jnp.inf, logits)
    m2 = jnp.max(rest, axis=-1, keepdims=True)
    i2 = jnp.min(jnp.where(rest == m2, lane, N_EXPERTS), axis=-1, keepdims=True)
    e2 = jnp.exp(m2 - m1)
    w1 = 1.0 / (1.0 + e2)
    first = lax.broadcasted_iota(jnp.int32, idx_ref.shape, 1) == 0
    idx_ref[...] = jnp.where(first, i1, i2)
    w_ref[...] = jnp.where(first, w1, e2 * w1)


def _moe_route(x2d, g, w_router):
    t, d = x2d.shape
    ne = w_router.shape[1]
    tm = _tile(t, TM_ROUTE)
    return pl.pallas_call(
        _route_body,
        grid=(t // tm,),
        in_specs=[pl.BlockSpec((tm, d), lambda i: (i, 0)),
                  pl.BlockSpec((1, d), lambda i: (0, 0)),
                  pl.BlockSpec((d, ne), lambda i: (0, 0))],
        out_specs=[pl.BlockSpec((tm,) + ROW_TILE, lambda i: (i, 0, 0)),
                   pl.BlockSpec((tm, TOP_K), lambda i: (i, 0)),
                   pl.BlockSpec((tm, TOP_K), lambda i: (i, 0))],
        out_shape=[jax.ShapeDtypeStruct((t,) + ROW_TILE, F32),
                   jax.ShapeDtypeStruct((t, TOP_K), jnp.int32),
                   jax.ShapeDtypeStruct((t, TOP_K), F32)],
        compiler_params=_cparams("parallel"),
        name="moe_route",
    )(x2d, g.reshape(1, d), w_router)


def _moe_plan(idx, tm):
    t = idx.shape[0]
    e = idx.reshape(-1)
    onehot = (e[:, None] == jnp.arange(N_EXPERTS, dtype=jnp.int32)[None, :]).astype(jnp.int32)
    csum = jnp.cumsum(onehot, axis=0)
    rank = jnp.sum((csum - onehot) * onehot, axis=1)
    tiles = (csum[-1] + tm - 1) // tm
    tile_end = jnp.cumsum(tiles)
    row0 = (tile_end - tiles) * tm
    dest = jnp.sum(onehot * row0[None, :], axis=1) + rank
    n_tiles = (TOP_K * t) // tm + N_EXPERTS
    row_token = jnp.zeros((n_tiles * tm,), jnp.int32).at[dest].set(
        jnp.arange(TOP_K * t, dtype=jnp.int32) // TOP_K, unique_indices=True)
    tile_expert = jnp.sum(jnp.arange(n_tiles, dtype=jnp.int32)[:, None] >= tile_end[None, :], axis=1)
    tile_expert = jnp.minimum(tile_expert, N_EXPERTS - 1).astype(jnp.int32)
    return row_token.reshape(n_tiles, 1, tm), dest.reshape(t, TOP_K), tile_expert, tile_end[-1:].astype(jnp.int32)


def _gather_rows(idx_ref, src_hbm, buf, sem, lo, n):
    def body(p, carry):
        rows = [lo + GATHER_UNROLL * p + q for q in range(GATHER_UNROLL)]
        toks = [idx_ref[0, 0, r] for r in rows]
        for q, (r, tok) in enumerate(zip(rows, toks)):
            pltpu.make_async_copy(src_hbm.at[pl.ds(tok, 1)], buf.at[pl.ds(r, 1)], sem).start(priority=q % 2)
        return carry

    lax.fori_loop(0, n // GATHER_UNROLL, body, 0)


def _wait_rows(src_hbm, buf, sem, n):
    pltpu.make_async_copy(src_hbm.at[pl.ds(0, n)], buf, sem).wait()


def _moe_ffn_body(te_ref, nu_ref, cur_ref, nxt_ref, h_hbm, wg_ref, wu_ref, wd_ref, o_ref, xbuf, sem,
                  wgb, wub, wdb, *, tm):
    i = pl.program_id(0)
    slot = i % 2
    n_used = nu_ref[0]

    @pl.when(jnp.logical_and(i == 0, n_used > 0))
    def _():
        _gather_rows(cur_ref, h_hbm, xbuf.at[0], sem.at[0], 0, tm)

    @pl.when(i + 1 < n_used)
    def _():
        _gather_rows(nxt_ref, h_hbm, xbuf.at[1 - slot], sem.at[1 - slot], 0, tm)

    new_expert = jnp.logical_or(i == 0, te_ref[i] != te_ref[jnp.maximum(i - 1, 0)])

    @pl.when(jnp.logical_and(i < n_used, new_expert))
    def _():
        wgb[...] = wg_ref[0].astype(BF16)
        wub[...] = wu_ref[0].astype(BF16)
        wdb[...] = wd_ref[0].astype(BF16)

    @pl.when(i < n_used)
    def _():
        _wait_rows(h_hbm, xbuf.at[slot], sem.at[slot], tm)
        xs = _load_row_tiles(xbuf.at[slot]).astype(BF16)
        a = jnp.dot(xs, wgb[...], preferred_element_type=F32)
        u = jnp.dot(xs, wub[...], preferred_element_type=F32)
        act = (a * jax.nn.sigmoid(a) * u).astype(BF16)
        _store_row_tiles(o_ref, jnp.dot(act, wdb[...], preferred_element_type=F32))

    @pl.when(i >= n_used)
    def _():
        o_ref[...] = jnp.zeros_like(o_ref)


def _moe_ffn(h, row_token, tile_expert, n_used, w_gate, w_up, w_down):
    ne, d, fe = w_gate.shape
    assert h.shape[1:] == ROW_TILE and d == ROW_TILE[0] * ROW_TILE[1]
    n_tiles, _, tm = row_token.shape
    grid_spec = pltpu.PrefetchScalarGridSpec(
        num_scalar_prefetch=2,
        grid=(n_tiles,),
        in_specs=[pl.BlockSpec((1, 1, tm), lambda i, te, nu: (i, 0, 0), memory_space=pltpu.SMEM),
                  pl.BlockSpec((1, 1, tm), lambda i, te, nu: (jnp.minimum(i + 1, n_tiles - 1), 0, 0),
                               memory_space=pltpu.SMEM),
                  pl.BlockSpec(memory_space=pl.ANY),
                  pl.BlockSpec((1, d, fe), lambda i, te, nu: (te[i], 0, 0), pipeline_mode=pl.Buffered(1)),
                  pl.BlockSpec((1, d, fe), lambda i, te, nu: (te[i], 0, 0), pipeline_mode=pl.Buffered(1)),
                  pl.BlockSpec((1, fe, d), lambda i, te, nu: (te[i], 0, 0), pipeline_mode=pl.Buffered(1))],
        out_specs=pl.BlockSpec((tm,) + ROW_TILE, lambda i, te, nu: (i, 0, 0)),
        scratch_shapes=[pltpu.VMEM((2, tm) + ROW_TILE, F32), pltpu.SemaphoreType.DMA((2,)),
                        pltpu.VMEM((d, fe), BF16), pltpu.VMEM((d, fe), BF16), pltpu.VMEM((fe, d), BF16)])
    return pl.pallas_call(
        functools.partial(_moe_ffn_body, tm=tm),
        grid_spec=grid_spec,
        out_shape=jax.ShapeDtypeStruct((n_tiles * tm,) + ROW_TILE, F32),
        compiler_params=_cparams("arbitrary"),
        name="moe_ffn",
    )(tile_expert, n_used, row_token, row_token, h, w_gate, w_up, w_down)


def _moe_combine_body(cur_ref, nxt_ref, x_ref, w_ref, y_hbm, o_ref, ybuf, sem, *, tc):
    i = pl.program_id(0)
    slot = i % 2
    n = TOP_K * tc

    @pl.when(i == 0)
    def _():
        _gather_rows(cur_ref, y_hbm, ybuf.at[0], sem.at[0], 0, n)

    @pl.when(i + 1 < pl.num_programs(0))
    def _():
        _gather_rows(nxt_ref, y_hbm, ybuf.at[1 - slot], sem.at[1 - slot], 0, n)

    _wait_rows(y_hbm, ybuf.at[slot], sem.at[slot], n)
    w = w_ref[...]
    y0 = _load_row_tiles(ybuf.at[slot, pl.ds(0, tc)])
    y1 = _load_row_tiles(ybuf.at[slot, pl.ds(tc, tc)])
    o_ref[...] = x_ref[...] + w[:, 0:1] * y0 + w[:, 1:2] * y1


def _moe_combine(x2d, ys, dest, w):
    t, d = x2d.shape
    tc = _tile(t, TM_COMBINE)
    nt = t // tc
    dest_tiles = dest.reshape(nt, tc, TOP_K).transpose(0, 2, 1).reshape(nt, 1, TOP_K * tc)
    return pl.pallas_call(
        functools.partial(_moe_combine_body, tc=tc),
        grid=(nt,),
        in_specs=[pl.BlockSpec((1, 1, TOP_K * tc), lambda i: (i, 0, 0), memory_space=pltpu.SMEM),
                  pl.BlockSpec((1, 1, TOP_K * tc), lambda i: (jnp.minimum(i + 1, nt - 1), 0, 0),
                               memory_space=pltpu.SMEM),
                  pl.BlockSpec((tc, d), lambda i: (i, 0)),
                  pl.BlockSpec((tc, TOP_K), lambda i: (i, 0)),
                  pl.BlockSpec(memory_space=pl.ANY)],
        out_specs=pl.BlockSpec((tc, d), lambda i: (i, 0)),
        out_shape=jax.ShapeDtypeStruct((t, d), F32),
        scratch_shapes=[pltpu.VMEM((2, TOP_K * tc) + ROW_TILE, F32), pltpu.SemaphoreType.DMA((2,))],
        compiler_params=_cparams("arbitrary"),
        name="moe_combine",
    )(dest_tiles, dest_tiles, x2d, w, ys)


def _moe(x2d, g, w_router, w_gate, w_up, w_down):
    h, idx, w = _moe_route(x2d, g, w_router)
    row_token, dest, tile_expert, n_used = _moe_plan(idx, _tile(TOP_K * x2d.shape[0], TM_MOE))
    ys = _moe_ffn(h, row_token, tile_expert, n_used, w_gate, w_up, w_down)
    return _moe_combine(x2d, ys, dest, w)


def _permute_w_in(w):
    off = np.concatenate([[0], np.cumsum(IN_SPLITS)]).tolist()
    piece = lambda i: w[:, off[i]:off[i + 1]].astype(BF16)
    order = [9, 0, 1, 3, 4, 5, 7, 8, 2, 6]
    pad = jnp.zeros((w.shape[0], D_PROJ - off[-1]), BF16)
    return jnp.concatenate([piece(i) for i in order] + [pad], axis=1)


def kernel(x, positions, g_mix, w_in, g_cq, w_uq, g_ckv, w_ukv, g_qk_q, g_qk_k, w_gla_gate_up, b_gla_gate, g_gla_out, w_pool, pool_scale, w_branch, w_out, g_ffn, w_ffn_gate, w_ffn_up, w_ffn_down, w_router, w_exp_gate, w_exp_up, w_exp_down):
    b, s, d = x.shape
    depth = g_mix.shape[0]
    for layer in range(depth):
        proj = _norm_matmul(x.reshape(b * s, d), g_mix[layer], _permute_w_in(w_in[layer]), BF16).reshape(b, s, D_PROJ)
        qT, k, vT = _mla_prep(proj, positions, g_cq[layer], w_uq[layer], g_ckv[layer], w_ukv[layer],
                              g_qk_q[layer], g_qk_k[layer])
        yaT = _attention(qT, k, vT).reshape(b, MLA_HEADS * MLA_V, s)
        yb = _gla(proj, w_gla_gate_up[layer], b_gla_gate[layer], g_gla_out[layer])
        yc = _pool(proj, w_pool[layer], pool_scale[layer])
        x = _merge(x, yaT, yb, yc, proj, w_branch[layer], w_out[layer])
        i = layer // 2
        if layer % 2 == 0:
            x2d = _ffn(x.reshape(b * s, d), g_ffn[layer], w_ffn_gate[i], w_ffn_up[i], w_ffn_down[i])
        else:
            x2d = _moe(x.reshape(b * s, d), g_ffn[layer], w_router[i], w_exp_gate[i], w_exp_up[i], w_exp_down[i])
        x = x2d.reshape(b, s, d)
    return x
```

```python
import functools

import jax
import jax.numpy as jnp
import numpy as np
from jax import lax
from jax.experimental import pallas as pl
from jax.experimental.pallas import tpu as pltpu

F32 = jnp.float32
BF16 = jnp.bfloat16
HIGHEST = lax.Precision.HIGHEST

D_MODEL = 1024
MLA_HEADS = 8
MLA_NOPE = 64
MLA_ROPE = 32
MLA_QK = MLA_NOPE + MLA_ROPE
MLA_V = 64
MLA_Q_RANK = D_MODEL // 4
MLA_KV_RANK = D_MODEL // 4
ROPE_BASE = 10000.0
GLA_HEADS = 4
GLA_DK = 64
GLA_DV = 128
GLA_GATE_RANK = 16
GLA_TAU = 16.0
POOL_WINDOWS = (2, 4, 8, 16)
POOL_GROUP = 128
POOL_WIDTH = 4 * POOL_GROUP
N_BRANCH = 3
BRANCH_WIDTH = 512
D_FF = 2816
N_EXPERTS = 8
TOP_K = 2
D_EXPERT = 1408
EPS = 1e-6

IN_SPLITS = (MLA_Q_RANK, MLA_KV_RANK, MLA_ROPE,
             GLA_HEADS * GLA_DK, GLA_HEADS * GLA_DK, GLA_HEADS * GLA_DV, GLA_GATE_RANK, GLA_HEADS * GLA_DV,
             POOL_WIDTH, N_BRANCH * D_MODEL)

LANES = 128
HEAD_PAD = LANES

COL_GATES = 0
COL_CQ = 3072
COL_CKV = 3328
COL_GQ = 3584
COL_GK = 3840
COL_GV = 4096
COL_GOG = 4608
COL_POOL = 5120
COL_SMALL = 5632
D_PROJ = 5760
SMALL_LR = MLA_ROPE

TM_PROJ = 1024
TN_PROJ = 1920
ATT_T = 512
ATT_NH = 4
V_ROWS = 80
GLA_L = 128
GLA_BASE = 1
GLA_NB = 4
TM_POOL = 256
POOL_HALO = 16
TM_MERGE = 512
TM_FFN = 1024
TF_FFN = 256
TM_ROUTE = 1024
TM_MOE = 512
TM_COMBINE = 512
GATHER_UNROLL = 16
VMEM_LIMIT = 56 * 1024 * 1024


def _cparams(*sem):
    return pltpu.CompilerParams(dimension_semantics=sem, vmem_limit_bytes=VMEM_LIMIT)


def _tile(n, pref):
    t = min(n, pref)
    assert n % t == 0, (n, pref)
    return t


def _rms(x):
    return x * lax.rsqrt(jnp.mean(x * x, axis=-1, keepdims=True) + EPS)


_NT = (((1,), (1,)), ((), ()))
_TN = (((0,), (0,)), ((), ()))


def _norm_matmul_body(x_ref, g_ref, w_ref, o_ref, h_ref):
    @pl.when(pl.program_id(1) == 0)
    def _():
        h_ref[...] = (_rms(x_ref[...]) * g_ref[...]).astype(BF16)

    o_ref[...] = jnp.dot(h_ref[...], w_ref[...], preferred_element_type=F32).astype(o_ref.dtype)


def _norm_matmul(x2d, g, w, out_dtype):
    t, d = x2d.shape
    n = w.shape[1]
    tm, tn = _tile(t, TM_PROJ), _tile(n, TN_PROJ)
    return pl.pallas_call(
        _norm_matmul_body,
        grid=(t // tm, n // tn),
        in_specs=[pl.BlockSpec((tm, d), lambda i, j: (i, 0)),
                  pl.BlockSpec((1, d), lambda i, j: (0, 0)),
                  pl.BlockSpec((d, tn), lambda i, j: (0, j))],
        out_specs=pl.BlockSpec((tm, tn), lambda i, j: (i, j)),
        out_shape=jax.ShapeDtypeStruct((t, n), out_dtype),
        scratch_shapes=[pltpu.VMEM((tm, d), BF16)],
        compiler_params=_cparams("parallel", "arbitrary"),
        name="norm_in_proj",
    )(x2d, g.reshape(1, d), w)


def _mla_prep_body(pos_ref, cq_ref, ckv_ref, small_ref, gcq_ref, wuq_ref, gckv_ref, wukv_ref, gq_ref, gk_ref,
                   invf_ref, qT_ref, k_ref, vT_ref):
    tp = cq_ref.shape[1]
    half = MLA_ROPE // 2
    ang = invf_ref[...] * pos_ref[0].astype(F32)
    cos, sin = jnp.cos(ang), jnp.sin(ang)

    def rope(t):
        x1, x2 = t[:half], t[half:]
        return x1 * cos - x2 * sin, x2 * cos + x1 * sin

    cqn = (_rms(cq_ref[0].astype(F32)) * gcq_ref[...]).astype(BF16)
    ckvn = (_rms(ckv_ref[0].astype(F32)) * gckv_ref[...]).astype(BF16)
    qT = lax.dot_general(wuq_ref[...], cqn, _NT, preferred_element_type=F32)
    kvT = lax.dot_general(wukv_ref[...], ckvn, _NT, preferred_element_type=F32)
    kr = small_ref[0].astype(F32).T[:MLA_ROPE]
    kr_ss = jnp.sum(kr * kr, axis=0, keepdims=True)
    gq, gk = gq_ref[...], gk_ref[...]
    zpad = jnp.zeros((HEAD_PAD - MLA_QK, tp), F32)
    vrow = lax.broadcasted_iota(jnp.int32, (V_ROWS - MLA_V, tp), 0)
    vpad = jnp.where(vrow == 0, 1.0, 0.0).astype(F32)
    scale = MLA_QK ** -0.5 * np.log2(np.e)
    for h in range(MLA_HEADS):
        qh = qT[h * MLA_QK:(h + 1) * MLA_QK]
        qn = qh * lax.rsqrt(jnp.mean(qh * qh, axis=0, keepdims=True) + EPS) * (gq * scale)
        r1, r2 = rope(qn[MLA_NOPE:])
        qT_ref[0, h] = jnp.concatenate([qn[:MLA_NOPE], r1, r2, zpad], axis=0).astype(BF16)
        base = h * (MLA_NOPE + MLA_V)
        kn = kvT[base:base + MLA_NOPE]
        r = lax.rsqrt((jnp.sum(kn * kn, axis=0, keepdims=True) + kr_ss) * (1.0 / MLA_QK) + EPS)
        r1, r2 = rope(kr * r * gk[MLA_NOPE:])
        kfm = jnp.concatenate([kn * r * gk[:MLA_NOPE], r1, r2, zpad], axis=0)
        k_ref[0, h] = kfm.T.astype(BF16)
        vT_ref[0, h, 0] = jnp.concatenate([kvT[base + MLA_NOPE:base + MLA_NOPE + MLA_V], vpad], axis=0).astype(BF16)


def _mla_prep(proj, positions, g_cq, w_uq, g_ckv, w_ukv, g_qk_q, g_qk_k):
    b, s, _ = proj.shape
    tp = _tile(s, ATT_T)
    ns = s // tp
    h = MLA_HEADS
    pos = positions.reshape(b * ns, 1, tp)
    inv_freq = (ROPE_BASE ** (-jnp.arange(MLA_ROPE // 2, dtype=F32) / (MLA_ROPE // 2))).reshape(-1, 1)
    const = lambda shape: pl.BlockSpec(shape, lambda bi, si: (0,) * len(shape))
    return pl.pallas_call(
        _mla_prep_body,
        grid=(b, ns),
        in_specs=[pl.BlockSpec((1, 1, tp), lambda bi, si: (bi * ns + si, 0, 0)),
                  pl.BlockSpec((1, tp, MLA_Q_RANK), lambda bi, si: (bi, si, COL_CQ // MLA_Q_RANK)),
                  pl.BlockSpec((1, tp, MLA_KV_RANK), lambda bi, si: (bi, si, COL_CKV // MLA_KV_RANK)),
                  pl.BlockSpec((1, tp, LANES), lambda bi, si: (bi, si, COL_SMALL // LANES)),
                  const((1, MLA_Q_RANK)), const((h * MLA_QK, MLA_Q_RANK)),
                  const((1, MLA_KV_RANK)), const((h * (MLA_NOPE + MLA_V), MLA_KV_RANK)),
                  const((MLA_QK, 1)), const((MLA_QK, 1)), const((MLA_ROPE // 2, 1))],
        out_specs=[pl.BlockSpec((1, h, HEAD_PAD, tp), lambda bi, si: (bi, 0, 0, si)),
                   pl.BlockSpec((1, h, tp, HEAD_PAD), lambda bi, si: (bi, 0, si, 0)),
                   pl.BlockSpec((1, h, 1, V_ROWS, tp), lambda bi, si: (bi, 0, si, 0, 0))],
        out_shape=[jax.ShapeDtypeStruct((b, h, HEAD_PAD, s), BF16),
                   jax.ShapeDtypeStruct((b, h, s, HEAD_PAD), BF16),
                   jax.ShapeDtypeStruct((b, h, ns, V_ROWS, tp), BF16)],
        compiler_params=_cparams("parallel", "parallel"),
        name="mla_prep",
    )(pos, proj, proj, proj, g_cq.reshape(1, -1), w_uq.T.astype(BF16), g_ckv.reshape(1, -1), w_ukv.T.astype(BF16),
      g_qk_q.reshape(-1, 1), g_qk_k.reshape(-1, 1), inv_freq)


def _attn_body(qT_ref, k_ref, vT_ref, o_ref, *, tq, nh):
    i = pl.program_id(2)

    def step(j, carry, masked):
        scores = []
        for h in range(nh):
            kb = k_ref[0, h, pl.ds(pl.multiple_of(j * tq, tq), tq), :]
            s = jnp.dot(kb, qT_ref[0, h], preferred_element_type=F32)
            if masked:
                kpos = lax.broadcasted_iota(jnp.int32, (tq, tq), 0)
                qpos = lax.broadcasted_iota(jnp.int32, (tq, tq), 1)
                s = jnp.where(kpos <= qpos, s, -jnp.inf)
            scores.append(s)
        out = []
        for h in range(nh):
            m, acc = carry[h]
            m_new = jnp.maximum(m, jnp.max(scores[h], axis=0, keepdims=True))
            p = jnp.exp2(scores[h] - m_new).astype(BF16)
            acc = jnp.exp2(m - m_new) * acc + jnp.dot(vT_ref[0, h, j], p, preferred_element_type=F32)
            out.append((m_new, acc))
        return tuple(out)

    carry = tuple((jnp.full((1, tq), -jnp.inf, F32), jnp.zeros((V_ROWS, tq), F32)) for _ in range(nh))
    carry = lax.fori_loop(0, i, functools.partial(step, masked=False), carry)
    carry = step(i, carry, True)
    for h in range(nh):
        acc = carry[h][1]
        o_ref[0, h] = (acc[:MLA_V] / acc[MLA_V:MLA_V + 1]).astype(o_ref.dtype)


def _attention(qT, k, vT):
    b, h, _, s = qT.shape
    tq = vT.shape[-1]
    nk = s // tq
    nh = ATT_NH
    return pl.pallas_call(
        functools.partial(_attn_body, tq=tq, nh=nh),
        grid=(b, h // nh, s // tq),
        in_specs=[pl.BlockSpec((1, nh, HEAD_PAD, tq), lambda bi, hi, qi: (bi, hi, 0, qi)),
                  pl.BlockSpec((1, nh, s, HEAD_PAD), lambda bi, hi, qi: (bi, hi, 0, 0)),
                  pl.BlockSpec((1, nh, nk, V_ROWS, tq), lambda bi, hi, qi: (bi, hi, 0, 0, 0))],
        out_specs=pl.BlockSpec((1, nh, MLA_V, tq), lambda bi, hi, qi: (bi, hi, 0, qi)),
        out_shape=jax.ShapeDtypeStruct((b, h, MLA_V, s), BF16),
        compiler_params=_cparams("parallel", "parallel", "arbitrary"),
        name="mla_attention",
    )(qT, k, vT)


def _gla_body(q_ref, k_ref, v_ref, og_ref, small_ref, wg_ref, bg_ref, gout_ref, o_ref, st_ref, *, blk):
    @pl.when(pl.program_id(1) == 0)
    def _():
        st_ref[...] = jnp.zeros_like(st_ref)

    hd = GLA_HEADS * GLA_DK
    nb = q_ref.shape[0]
    row = lax.broadcasted_iota(jnp.int32, (blk, blk), 0)
    col = lax.broadcasted_iota(jnp.int32, (blk, blk), 1)
    tril = (col <= row).astype(F32)

    def prepare(bb):
        lr = small_ref[bb][:, SMALL_LR:SMALL_LR + GLA_GATE_RANK].astype(F32)
        z = jnp.dot(lr, wg_ref[...], precision=HIGHEST, preferred_element_type=F32) + bg_ref[...]
        log_a = jax.nn.log_sigmoid(z) * (1.0 / GLA_TAU)
        bc = jnp.dot(tril, log_a, precision=HIGHEST, preferred_element_type=F32)
        q = q_ref[bb].astype(F32) * (GLA_DK ** -0.5)
        k = k_ref[bb].astype(F32)

        def ref_rows(group, off):
            parts = [jnp.broadcast_to(bc[g0 + off:g0 + off + 1], (group, hd)) for g0 in range(0, blk, group)]
            return parts[0] if len(parts) == 1 else jnp.concatenate(parts, axis=0)

        levels = []
        group = blk
        while group > GLA_BASE:
            half = group // 2
            ref = ref_rows(group, half - 1)
            qs = (q * jnp.exp(jnp.minimum(bc - ref, 0.0))).astype(BF16)
            ks = (k * jnp.exp(jnp.minimum(ref - bc, 0.0))).astype(BF16)
            mask = ((row & -group) == (col & -group)) & ((row & half) != 0) & ((col & half) == 0)
            levels.append((qs, ks, mask))
            group = half
        if GLA_BASE == 1:
            levels.append((q.astype(BF16), k.astype(BF16), row == col))
        else:
            ref = ref_rows(GLA_BASE, GLA_BASE // 2 - 1)
            mask = ((row & -GLA_BASE) == (col & -GLA_BASE)) & (col <= row)
            levels.append(((q * jnp.exp(bc - ref)).astype(BF16), (k * jnp.exp(ref - bc)).astype(BF16), mask))
        b_last = bc[blk - 1:blk]
        return dict(levels=levels, qd=(q * jnp.exp(bc)).astype(BF16), kd=(k * jnp.exp(b_last - bc)).astype(BF16),
                    a_last=jnp.exp(b_last))

    prep = [prepare(bb) for bb in range(nb)]
    gout = gout_ref[...]
    for h in range(GLA_HEADS):
        ks_, vs_ = slice(h * GLA_DK, (h + 1) * GLA_DK), slice(h * GLA_DV, (h + 1) * GLA_DV)
        for bb in range(nb):
            pr = prep[bb]
            attn = jnp.zeros((blk, blk), F32)
            for qs, ksc, mask in pr["levels"]:
                p = lax.dot_general(qs[:, ks_], ksc[:, ks_], _NT, preferred_element_type=F32)
                attn = jnp.where(mask, p, attn)
            v_h = v_ref[bb, :, vs_]
            st = st_ref[bb, h]
            o = jnp.dot(attn.astype(BF16), v_h, preferred_element_type=F32)
            o = o + lax.dot_general(pr["qd"][:, ks_], st.astype(BF16), _NT, preferred_element_type=F32)
            st_ref[bb, h] = (st * pr["a_last"][:, ks_]
                             + lax.dot_general(v_h, pr["kd"][:, ks_], _TN, preferred_element_type=F32))
            og = og_ref[bb, :, vs_].astype(F32)
            o_ref[bb, :, vs_] = (_rms(o) * gout * (og * jax.nn.sigmoid(og))).astype(o_ref.dtype)


def _gla(proj, w_gate_up, b_gate, g_out):
    b, s, _ = proj.shape
    blk = _tile(s, GLA_L)
    nb = _tile(b, GLA_NB)
    hd, hv = GLA_HEADS * GLA_DK, GLA_HEADS * GLA_DV
    const = lambda shape: pl.BlockSpec(shape, lambda bi, si: (0,) * len(shape))
    return pl.pallas_call(
        functools.partial(_gla_body, blk=blk),
        grid=(b // nb, s // blk),
        in_specs=[pl.BlockSpec((nb, blk, hd), lambda bi, si: (bi, si, COL_GQ // hd)),
                  pl.BlockSpec((nb, blk, hd), lambda bi, si: (bi, si, COL_GK // hd)),
                  pl.BlockSpec((nb, blk, hv), lambda bi, si: (bi, si, COL_GV // hv)),
                  pl.BlockSpec((nb, blk, hv), lambda bi, si: (bi, si, COL_GOG // hv)),
                  pl.BlockSpec((nb, blk, LANES), lambda bi, si: (bi, si, COL_SMALL // LANES)),
                  const((GLA_GATE_RANK, hd)), const((1, hd)), const((1, GLA_DV))],
        out_specs=pl.BlockSpec((nb, blk, hv), lambda bi, si: (bi, si, 0)),
        out_shape=jax.ShapeDtypeStruct((b, s, hv), BF16),
        scratch_shapes=[pltpu.VMEM((nb, GLA_HEADS, GLA_DV, GLA_DK), F32)],
        compiler_params=_cparams("parallel", "arbitrary"),
        name="gla",
    )(proj, proj, proj, proj, proj, w_gate_up, b_gate.reshape(1, -1), g_out.reshape(1, -1))


def _pool_body(u_ref, halo_ref, w_ref, scale_ref, o_ref, *, tm):
    si = pl.program_id(1)
    row = lax.broadcasted_iota(jnp.int32, (tm, tm + POOL_HALO), 0)
    col = lax.broadcasted_iota(jnp.int32, (tm, tm + POOL_HALO), 1)
    t1 = (si * tm + lax.broadcasted_iota(jnp.int32, (tm, 1), 0) + 1).astype(F32)
    for g, win in enumerate(POOL_WINDOWS):
        cs = slice(g * POOL_GROUP, (g + 1) * POOL_GROUP)
        cur = u_ref[0, :, cs]
        halo = halo_ref[0, :, cs]
        ext = jnp.concatenate([jnp.where(si > 0, halo, jnp.zeros_like(halo)), cur], axis=0)
        band = ((col <= row + POOL_HALO) & (col > row + POOL_HALO - win)).astype(BF16)
        wsum = jnp.dot(band, ext, preferred_element_type=F32)
        pooled = wsum / jnp.minimum(t1, float(win)) - cur.astype(F32)
        y = jnp.dot(pooled.astype(BF16), w_ref[g], preferred_element_type=F32)
        o_ref[0, :, cs] = (y * scale_ref[:, cs]).astype(o_ref.dtype)


def _pool(proj, w_pool, pool_scale):
    b, s, _ = proj.shape
    tm = _tile(s, TM_POOL)
    hb = tm // POOL_HALO
    return pl.pallas_call(
        functools.partial(_pool_body, tm=tm),
        grid=(b, s // tm),
        in_specs=[pl.BlockSpec((1, tm, POOL_WIDTH), lambda bi, si: (bi, si, COL_POOL // POOL_WIDTH)),
                  pl.BlockSpec((1, POOL_HALO, POOL_WIDTH),
                               lambda bi, si: (bi, jnp.maximum(si * hb - 1, 0), COL_POOL // POOL_WIDTH)),
                  pl.BlockSpec((len(POOL_WINDOWS), POOL_GROUP, POOL_GROUP), lambda bi, si: (0, 0, 0)),
                  pl.BlockSpec((1, POOL_WIDTH), lambda bi, si: (0, 0))],
        out_specs=pl.BlockSpec((1, tm, POOL_WIDTH), lambda bi, si: (bi, si, 0)),
        out_shape=jax.ShapeDtypeStruct((b, s, POOL_WIDTH), BF16),
        compiler_params=_cparams("parallel", "parallel"),
        name="pool",
    )(proj, proj, w_pool.astype(BF16), pool_scale.reshape(1, -1))


def _merge_body(x_ref, yaT_ref, yb_ref, yc_ref, g0_ref, g1_ref, g2_ref, wb_ref, wo_ref, o_ref):
    d_a = lax.dot_general(yaT_ref[0], wb_ref[0], _TN, preferred_element_type=F32)
    d_b = jnp.dot(yb_ref[0], wb_ref[1], preferred_element_type=F32)
    d_c = jnp.dot(yc_ref[0], wb_ref[2], preferred_element_type=F32)
    merged = (jax.nn.sigmoid(g0_ref[0].astype(F32)) * d_a + jax.nn.sigmoid(g1_ref[0].astype(F32)) * d_b
              + jax.nn.sigmoid(g2_ref[0].astype(F32)) * d_c)
    o_ref[0] = x_ref[0] + jnp.dot(merged.astype(BF16), wo_ref[...], preferred_element_type=F32)


def _merge(x, yaT, yb, yc, proj, w_branch, w_out):
    b, s, d = x.shape
    tm = _tile(s, TM_MERGE)
    tok = lambda width, cb: pl.BlockSpec((1, tm, width), lambda bi, si: (bi, si, cb))
    return pl.pallas_call(
        _merge_body,
        grid=(b, s // tm),
        in_specs=[tok(d, 0),
                  pl.BlockSpec((1, BRANCH_WIDTH, tm), lambda bi, si: (bi, 0, si)),
                  tok(BRANCH_WIDTH, 0), tok(BRANCH_WIDTH, 0),
                  tok(d, COL_GATES // d), tok(d, COL_GATES // d + 1), tok(d, COL_GATES // d + 2),
                  pl.BlockSpec((N_BRANCH, BRANCH_WIDTH, d), lambda bi, si: (0, 0, 0)),
                  pl.BlockSpec((d, d), lambda bi, si: (0, 0))],
        out_specs=tok(d, 0),
        out_shape=jax.ShapeDtypeStruct((b, s, d), F32),
        compiler_params=_cparams("parallel", "parallel"),
        name="merge_out_proj",
    )(x, yaT, yb, yc, proj, proj, proj, w_branch.astype(BF16), w_out.astype(BF16))


def _ffn_body(x_ref, g_ref, wg_ref, wu_ref, wd_ref, o_ref, act_ref, *, tf):
    h = (_rms(x_ref[...]) * g_ref[...]).astype(BF16)
    nf = act_ref.shape[0]
    for c in range(nf):
        a = jnp.dot(h, wg_ref[:, c * tf:(c + 1) * tf], preferred_element_type=F32)
        u = jnp.dot(h, wu_ref[:, c * tf:(c + 1) * tf], preferred_element_type=F32)
        act_ref[c] = (a * jax.nn.sigmoid(a) * u).astype(BF16)
    act = jnp.concatenate([act_ref[c] for c in range(nf)], axis=1)
    o_ref[...] = x_ref[...] + jnp.dot(act, wd_ref[...], preferred_element_type=F32)


def _ffn(x2d, g, w_gate, w_up, w_down):
    t, d = x2d.shape
    ff = w_gate.shape[1]
    tm, tf = _tile(t, TM_FFN), _tile(ff, TF_FFN)
    resident = lambda shape: pl.BlockSpec(shape, lambda i: (0, 0), pipeline_mode=pl.Buffered(1))
    return pl.pallas_call(
        functools.partial(_ffn_body, tf=tf),
        grid=(t // tm,),
        in_specs=[pl.BlockSpec((tm, d), lambda i: (i, 0)),
                  pl.BlockSpec((1, d), lambda i: (0, 0)),
                  resident((d, ff)), resident((d, ff)), resident((ff, d))],
        out_specs=pl.BlockSpec((tm, d), lambda i: (i, 0)),
        out_shape=jax.ShapeDtypeStruct((t, d), F32),
        scratch_shapes=[pltpu.VMEM((ff // tf, tm, tf), BF16)],
        compiler_params=_cparams("parallel"),
        name="ffn_dense",
    )(x2d, g.reshape(1, d), w_gate.astype(BF16), w_up.astype(BF16), w_down.astype(BF16))


ROW_TILE = (8, LANES)


def _store_row_tiles(ref, val):
    for c in range(ROW_TILE[0]):
        ref[:, c, :] = val[:, c * LANES:(c + 1) * LANES]


def _load_row_tiles(ref_view):
    return jnp.concatenate([ref_view[:, c, :] for c in range(ROW_TILE[0])], axis=1)


def _route_body(x_ref, g_ref, wr_ref, h_ref, idx_ref, w_ref):
    hf = _rms(x_ref[...]) * g_ref[...]
    _store_row_tiles(h_ref, hf)
    logits = jnp.dot(hf, wr_ref[...], precision=HIGHEST, preferred_element_type=F32)
    lane = lax.broadcasted_iota(jnp.int32, logits.shape, 1)
    m1 = jnp.max(logits, axis=-1, keepdims=True)
    i1 = jnp.min(jnp.where(logits == m1, lane, N_EXPERTS), axis=-1, keepdims=True)
    rest = jnp.where(lane == i1, -jnp.inf, logits)
    m2 = jnp.max(rest, axis=-1, keepdims=True)
    i2 = jnp.min(jnp.where(rest == m2, lane, N_EXPERTS), axis=-1, keepdims=True)
    e2 = jnp.exp(m2 - m1)
    w1 = 1.0 / (1.0 + e2)
    first = lax.broadcasted_iota(jnp.int32, idx_ref.shape, 1) == 0
    idx_ref[...] = jnp.where(first, i1, i2)
    w_ref[...] = jnp.where(first, w1, e2 * w1)


def _moe_route(x2d, g, w_router):
    t, d = x2d.shape
    ne = w_router.shape[1]
    tm = _tile(t, TM_ROUTE)
    return pl.pallas_call(
        _route_body,
        grid=(t // tm,),
        in_specs=[pl.BlockSpec((tm, d), lambda i: (i, 0)),
                  pl.BlockSpec((1, d), lambda i: (0, 0)),
                  pl.BlockSpec((d, ne), lambda i: (0, 0))],
        out_specs=[pl.BlockSpec((tm,) + ROW_TILE, lambda i: (i, 0, 0)),
                   pl.BlockSpec((tm, TOP_K), lambda i: (i, 0)),
                   pl.BlockSpec((tm, TOP_K), lambda i: (i, 0))],
        out_shape=[jax.ShapeDtypeStruct((t,) + ROW_TILE, F32),
                   jax.ShapeDtypeStruct((t, TOP_K), jnp.int32),
                   jax.ShapeDtypeStruct((t, TOP_K), F32)],
        compiler_params=_cparams("parallel"),
        name="moe_route",
    )(x2d, g.reshape(1, d), w_router)


def _moe_plan(idx, tm):
    t = idx.shape[0]
    e = idx.reshape(-1)
    onehot = (e[:, None] == jnp.arange(N_EXPERTS, dtype=jnp.int32)[None, :]).astype(jnp.int32)
    csum = jnp.cumsum(onehot, axis=0)
    rank = jnp.sum((csum - onehot) * onehot, axis=1)
    tiles = (csum[-1] + tm - 1) // tm
    tile_end = jnp.cumsum(tiles)
    row0 = (tile_end - tiles) * tm
    dest = jnp.sum(onehot * row0[None, :], axis=1) + rank
    n_tiles = (TOP_K * t) // tm + N_EXPERTS
    row_token = jnp.zeros((n_tiles * tm,), jnp.int32).at[dest].set(
        jnp.arange(TOP_K * t, dtype=jnp.int32) // TOP_K, unique_indices=True)
    tile_expert = jnp.sum(jnp.arange(n_tiles, dtype=jnp.int32)[:, None] >= tile_end[None, :], axis=1)
    tile_expert = jnp.minimum(tile_expert, N_EXPERTS - 1).astype(jnp.int32)
    return row_token.reshape(n_tiles, 1, tm), dest.reshape(t, TOP_K), tile_expert, tile_end[-1:].astype(jnp.int32)


def _gather_rows(idx_ref, src_hbm, buf, sem, lo, n):
    def body(p, carry):
        rows = [lo + GATHER_UNROLL * p + q for q in range(GATHER_UNROLL)]
        toks = [idx_ref[0, 0, r] for r in rows]
        for q, (r, tok) in enumerate(zip(rows, toks)):
            pltpu.make_async_copy(src_hbm.at[pl.ds(tok, 1)], buf.at[pl.ds(r, 1)], sem).start(priority=q % 2)
        return carry

    lax.fori_loop(0, n // GATHER_UNROLL, body, 0)


def _wait_rows(src_hbm, buf, sem, n):
    pltpu.make_async_copy(src_hbm.at[pl.ds(0, n)], buf, sem).wait()


def _moe_ffn_body(te_ref, nu_ref, cur_ref, nxt_ref, h_hbm, wg_ref, wu_ref, wd_ref, o_ref, xbuf, sem,
                  wgb, wub, wdb, *, tm):
    i = pl.program_id(0)
    slot = i % 2
    n_used = nu_ref[0]

    @pl.when(jnp.logical_and(i == 0, n_used > 0))
    def _():
        _gather_rows(cur_ref, h_hbm, xbuf.at[0], sem.at[0], 0, tm)

    @pl.when(i + 1 < n_used)
    def _():
        _gather_rows(nxt_ref, h_hbm, xbuf.at[1 - slot], sem.at[1 - slot], 0, tm)

    new_expert = jnp.logical_or(i == 0, te_ref[i] != te_ref[jnp.maximum(i - 1, 0)])

    @pl.when(jnp.logical_and(i < n_used, new_expert))
    def _():
        wgb[...] = wg_ref[0].astype(BF16)
        wub[...] = wu_ref[0].astype(BF16)
        wdb[...] = wd_ref[0].astype(BF16)

    @pl.when(i < n_used)
    def _():
        _wait_rows(h_hbm, xbuf.at[slot], sem.at[slot], tm)
        xs = _load_row_tiles(xbuf.at[slot]).astype(BF16)
        a = jnp.dot(xs, wgb[...], preferred_element_type=F32)
        u = jnp.dot(xs, wub[...], preferred_element_type=F32)
        act = (a * jax.nn.sigmoid(a) * u).astype(BF16)
        _store_row_tiles(o_ref, jnp.dot(act, wdb[...], preferred_element_type=F32))

    @pl.when(i >= n_used)
    def _():
        o_ref[...] = jnp.zeros_like(o_ref)


def _moe_ffn(h, row_token, tile_expert, n_used, w_gate, w_up, w_down):
    ne, d, fe = w_gate.shape
    assert h.shape[1:] == ROW_TILE and d == ROW_TILE[0] * ROW_TILE[1]
    n_tiles, _, tm = row_token.shape
    grid_spec = pltpu.PrefetchScalarGridSpec(
        num_scalar_prefetch=2,
        grid=(n_tiles,),
        in_specs=[pl.BlockSpec((1, 1, tm), lambda i, te, nu: (i, 0, 0), memory_space=pltpu.SMEM),
                  pl.BlockSpec((1, 1, tm), lambda i, te, nu: (jnp.minimum(i + 1, n_tiles - 1), 0, 0),
                               memory_space=pltpu.SMEM),
                  pl.BlockSpec(memory_space=pl.ANY),
                  pl.BlockSpec((1, d, fe), lambda i, te, nu: (te[i], 0, 0), pipeline_mode=pl.Buffered(1)),
                  pl.BlockSpec((1, d, fe), lambda i, te, nu: (te[i], 0, 0), pipeline_mode=pl.Buffered(1)),
                  pl.BlockSpec((1, fe, d), lambda i, te, nu: (te[i], 0, 0), pipeline_mode=pl.Buffered(1))],
        out_specs=pl.BlockSpec((tm,) + ROW_TILE, lambda i, te, nu: (i, 0, 0)),
        scratch_shapes=[pltpu.VMEM((2, tm) + ROW_TILE, F32), pltpu.SemaphoreType.DMA((2,)),
                        pltpu.VMEM((d, fe), BF16), pltpu.VMEM((d, fe), BF16), pltpu.VMEM((fe, d), BF16)])
    return pl.pallas_call(
        functools.partial(_moe_ffn_body, tm=tm),
        grid_spec=grid_spec,
        out_shape=jax.ShapeDtypeStruct((n_tiles * tm,) + ROW_TILE, F32),
        compiler_params=_cparams("arbitrary"),
        name="moe_ffn",
    )(tile_expert, n_used, row_token, row_token, h, w_gate, w_up, w_down)


def _moe_combine_body(cur_ref, nxt_ref, x_ref, w_ref, y_hbm, o_ref, ybuf, sem, *, tc):
    i = pl.program_id(0)
    slot = i % 2
    n = TOP_K * tc

    @pl.when(i == 0)
    def _():
        _gather_rows(cur_ref, y_hbm, ybuf.at[0], sem.at[0], 0, n)

    @pl.when(i + 1 < pl.num_programs(0))
    def _():
        _gather_rows(nxt_ref, y_hbm, ybuf.at[1 - slot], sem.at[1 - slot], 0, n)

    _wait_rows(y_hbm, ybuf.at[slot], sem.at[slot], n)
    w = w_ref[...]
    y0 = _load_row_tiles(ybuf.at[slot, pl.ds(0, tc)])
    y1 = _load_row_tiles(ybuf.at[slot, pl.ds(tc, tc)])
    o_ref[...] = x_ref[...] + w[:, 0:1] * y0 + w[:, 1:2] * y1


def _moe_combine(x2d, ys, dest, w):
    t, d = x2d.shape
    tc = _tile(t, TM_COMBINE)
    nt = t // tc
    dest_tiles = dest.reshape(nt, tc, TOP_K).transpose(0, 2, 1).reshape(nt, 1, TOP_K * tc)
    return pl.pallas_call(
        functools.partial(_moe_combine_body, tc=tc),
        grid=(nt,),
        in_specs=[pl.BlockSpec((1, 1, TOP_K * tc), lambda i: (i, 0, 0), memory_space=pltpu.SMEM),
                  pl.BlockSpec((1, 1, TOP_K * tc), lambda i: (jnp.minimum(i + 1, nt - 1), 0, 0),
                               memory_space=pltpu.SMEM),
                  pl.BlockSpec((tc, d), lambda i: (i, 0)),
                  pl.BlockSpec((tc, TOP_K), lambda i: (i, 0)),
                  pl.BlockSpec(memory_space=pl.ANY)],
        out_specs=pl.BlockSpec((tc, d), lambda i: (i, 0)),
        out_shape=jax.ShapeDtypeStruct((t, d), F32),
        scratch_shapes=[pltpu.VMEM((2, TOP_K * tc) + ROW_TILE, F32), pltpu.SemaphoreType.DMA((2,))],
        compiler_params=_cparams("arbitrary"),
        name="moe_combine",
    )(dest_tiles, dest_tiles, x2d, w, ys)


def _moe(x2d, g, w_router, w_gate, w_up, w_down):
    h, idx, w = _moe_route(x2d, g, w_router)
    row_token, dest, tile_expert, n_used = _moe_plan(idx, _tile(TOP_K * x2d.shape[0], TM_MOE))
    ys = _moe_ffn(h, row_token, tile_expert, n_used, w_gate, w_up, w_down)
    return _moe_combine(x2d, ys, dest, w)


def _permute_w_in(w):
    off = np.concatenate([[0], np.cumsum(IN_SPLITS)]).tolist()
    piece = lambda i: w[:, off[i]:off[i + 1]].astype(BF16)
    order = [9, 0, 1, 3, 4, 5, 7, 8, 2, 6]
    pad = jnp.zeros((w.shape[0], D_PROJ - off[-1]), BF16)
    return jnp.concatenate([piece(i) for i in order] + [pad], axis=1)


def kernel(x, positions, g_mix, w_in, g_cq, w_uq, g_ckv, w_ukv, g_qk_q, g_qk_k, w_gla_gate_up, b_gla_gate, g_gla_out, w_pool, pool_scale, w_branch, w_out, g_ffn, w_ffn_gate, w_ffn_up, w_ffn_down, w_router, w_exp_gate, w_exp_up, w_exp_down):
    b, s, d = x.shape
    depth = g_mix.shape[0]
    for layer in range(depth):
        proj = _norm_matmul(x.reshape(b * s, d), g_mix[layer], _permute_w_in(w_in[layer]), BF16).reshape(b, s, D_PROJ)
        qT, k, vT = _mla_prep(proj, positions, g_cq[layer], w_uq[layer], g_ckv[layer], w_ukv[layer],
                              g_qk_q[layer], g_qk_k[layer])
        yaT = _attention(qT, k, vT).reshape(b, MLA_HEADS * MLA_V, s)
        yb = _gla(proj, w_gla_gate_up[layer], b_gla_gate[layer], g_gla_out[layer])
        yc = _pool(proj, w_pool[layer], pool_scale[layer])
        x = _merge(x, yaT, yb, yc, proj, w_branch[layer], w_out[layer])
        i = layer // 2
        if layer % 2 == 0:
            x2d = _ffn(x.reshape(b * s, d), g_ffn[layer], w_ffn_gate[i], w_ffn_up[i], w_ffn_down[i])
        else:
            x2d = _moe(x.reshape(b * s, d), g_ffn[layer], w_router[i], w_exp_gate[i], w_exp_up[i], w_exp_down[i])
        x = x2d.reshape(b, s, d)
    return x
```

```python
import functools

import jax
import jax.numpy as jnp
import numpy as np
from jax import lax
from jax.experimental import pallas as pl
from jax.experimental.pallas import tpu as pltpu

F32 = jnp.float32
BF16 = jnp.bfloat16
HIGHEST = lax.Precision.HIGHEST

D_MODEL = 1024
MLA_HEADS = 8
MLA_NOPE = 64
MLA_ROPE = 32
MLA_QK = MLA_NOPE + MLA_ROPE
MLA_V = 64
MLA_Q_RANK = D_MODEL // 4
MLA_KV_RANK = D_MODEL // 4
ROPE_BASE = 10000.0
GLA_HEADS = 4
GLA_DK = 64
GLA_DV = 128
GLA_GATE_RANK = 16
GLA_TAU = 16.0
POOL_WINDOWS = (2, 4, 8, 16)
POOL_GROUP = 128
POOL_WIDTH = 4 * POOL_GROUP
N_BRANCH = 3
BRANCH_WIDTH = 512
D_FF = 2816
N_EXPERTS = 8
TOP_K = 2
D_EXPERT = 1408
EPS = 1e-6

IN_SPLITS = (MLA_Q_RANK, MLA_KV_RANK, MLA_ROPE,
             GLA_HEADS * GLA_DK, GLA_HEADS * GLA_DK, GLA_HEADS * GLA_DV, GLA_GATE_RANK, GLA_HEADS * GLA_DV,
             POOL_WIDTH, N_BRANCH * D_MODEL)

LANES = 128
HEAD_PAD = LANES

COL_GATES = 0
COL_CQ = 3072
COL_CKV = 3328
COL_GQ = 3584
COL_GK = 3840
COL_GV = 4096
COL_GOG = 4608
COL_POOL = 5120
COL_SMALL = 5632
D_PROJ = 5760
SMALL_LR = MLA_ROPE

TM_PROJ = 1024
TN_PROJ = 1536
ATT_T = 512
ATT_NH = 4
V_ROWS = 80
GLA_L = 128
GLA_BASE = 1
GLA_NB = 4
TM_POOL = 256
POOL_HALO = 16
TM_MERGE = 512
TM_FFN = 1024
TF_FFN = 256
TM_ROUTE = 1024
TM_MOE = 512
TM_COMBINE = 512
GATHER_UNROLL = 16
VMEM_LIMIT = 56 * 1024 * 1024


def _cparams(*sem):
    return pltpu.CompilerParams(dimension_semantics=sem, vmem_limit_bytes=VMEM_LIMIT)


def _tile(n, pref):
    t = min(n, pref)
    assert n % t == 0, (n, pref)
    return t


def _rms(x):
    return x * lax.rsqrt(jnp.mean(x * x, axis=-1, keepdims=True) + EPS)


_NT = (((1,), (1,)), ((), ()))
_TN = (((0,), (0,)), ((), ()))


def _norm_matmul_body(x_ref, g_ref, w_ref, o_ref):
    h = (_rms(x_ref[...]) * g_ref[...]).astype(BF16)
    n = o_ref.shape[1]
    for lo in range(0, n, TN_PROJ):
        hi = min(lo + TN_PROJ, n)
        o_ref[:, lo:hi] = jnp.dot(h, w_ref[:, lo:hi], preferred_element_type=F32).astype(o_ref.dtype)


def _norm_matmul(x2d, g, w, out_dtype):
    t, d = x2d.shape
    n = w.shape[1]
    tm = _tile(t, TM_PROJ)
    return pl.pallas_call(
        _norm_matmul_body,
        grid=(t // tm,),
        in_specs=[pl.BlockSpec((tm, d), lambda i: (i, 0)),
                  pl.BlockSpec((1, d), lambda i: (0, 0)),
                  pl.BlockSpec((d, n), lambda i: (0, 0), pipeline_mode=pl.Buffered(1))],
        out_specs=pl.BlockSpec((tm, n), lambda i: (i, 0)),
        out_shape=jax.ShapeDtypeStruct((t, n), out_dtype),
        compiler_params=_cparams("parallel"),
        name="norm_in_proj",
    )(x2d, g.reshape(1, d), w)


def _mla_prep_body(pos_ref, cq_ref, ckv_ref, small_ref, gcq_ref, wuq_ref, gckv_ref, wukv_ref, gq_ref, gk_ref,
                   invf_ref, qT_ref, k_ref, vT_ref):
    tp = cq_ref.shape[1]
    half = MLA_ROPE // 2
    ang = invf_ref[...] * pos_ref[0].astype(F32)
    cos, sin = jnp.cos(ang), jnp.sin(ang)

    def rope(t):
        x1, x2 = t[:half], t[half:]
        return x1 * cos - x2 * sin, x2 * cos + x1 * sin

    cqn = (_rms(cq_ref[0].astype(F32)) * gcq_ref[...]).astype(BF16)
    ckvn = (_rms(ckv_ref[0].astype(F32)) * gckv_ref[...]).astype(BF16)
    qT = lax.dot_general(wuq_ref[...], cqn, _NT, preferred_element_type=F32)
    kvT = lax.dot_general(wukv_ref[...], ckvn, _NT, preferred_element_type=F32)
    kr = small_ref[0].astype(F32).T[:MLA_ROPE]
    kr_ss = jnp.sum(kr * kr, axis=0, keepdims=True)
    gq, gk = gq_ref[...], gk_ref[...]
    zpad = jnp.zeros((HEAD_PAD - MLA_QK, tp), F32)
    vrow = lax.broadcasted_iota(jnp.int32, (V_ROWS - MLA_V, tp), 0)
    vpad = jnp.where(vrow == 0, 1.0, 0.0).astype(F32)
    scale = MLA_QK ** -0.5 * np.log2(np.e)
    for h in range(MLA_HEADS):
        qh = qT[h * MLA_QK:(h + 1) * MLA_QK]
        qn = qh * lax.rsqrt(jnp.mean(qh * qh, axis=0, keepdims=True) + EPS) * (gq * scale)
        r1, r2 = rope(qn[MLA_NOPE:])
        qT_ref[0, h] = jnp.concatenate([qn[:MLA_NOPE], r1, r2, zpad], axis=0).astype(BF16)
        base = h * (MLA_NOPE + MLA_V)
        kn = kvT[base:base + MLA_NOPE]
        r = lax.rsqrt((jnp.sum(kn * kn, axis=0, keepdims=True) + kr_ss) * (1.0 / MLA_QK) + EPS)
        r1, r2 = rope(kr * r * gk[MLA_NOPE:])
        kfm = jnp.concatenate([kn * r * gk[:MLA_NOPE], r1, r2, zpad], axis=0)
        k_ref[0, h] = kfm.T.astype(BF16)
        vT_ref[0, h, 0] = jnp.concatenate([kvT[base + MLA_NOPE:base + MLA_NOPE + MLA_V], vpad], axis=0).astype(BF16)


def _mla_prep(proj, positions, g_cq, w_uq, g_ckv, w_ukv, g_qk_q, g_qk_k):
    b, s, _ = proj.shape
    tp = _tile(s, ATT_T)
    ns = s // tp
    h = MLA_HEADS
    pos = positions.reshape(b * ns, 1, tp)
    inv_freq = (ROPE_BASE ** (-jnp.arange(MLA_ROPE // 2, dtype=F32) / (MLA_ROPE // 2))).reshape(-1, 1)
    const = lambda shape: pl.BlockSpec(shape, lambda bi, si: (0,) * len(shape))
    return pl.pallas_call(
        _mla_prep_body,
        grid=(b, ns),
        in_specs=[pl.BlockSpec((1, 1, tp), lambda bi, si: (bi * ns + si, 0, 0)),
                  pl.BlockSpec((1, tp, MLA_Q_RANK), lambda bi, si: (bi, si, COL_CQ // MLA_Q_RANK)),
                  pl.BlockSpec((1, tp, MLA_KV_RANK), lambda bi, si: (bi, si, COL_CKV // MLA_KV_RANK)),
                  pl.BlockSpec((1, tp, LANES), lambda bi, si: (bi, si, COL_SMALL // LANES)),
                  const((1, MLA_Q_RANK)), const((h * MLA_QK, MLA_Q_RANK)),
                  const((1, MLA_KV_RANK)), const((h * (MLA_NOPE + MLA_V), MLA_KV_RANK)),
                  const((MLA_QK, 1)), const((MLA_QK, 1)), const((MLA_ROPE // 2, 1))],
        out_specs=[pl.BlockSpec((1, h, HEAD_PAD, tp), lambda bi, si: (bi, 0, 0, si)),
                   pl.BlockSpec((1, h, tp, HEAD_PAD), lambda bi, si: (bi, 0, si, 0)),
                   pl.BlockSpec((1, h, 1, V_ROWS, tp), lambda bi, si: (bi, 0, si, 0, 0))],
        out_shape=[jax.ShapeDtypeStruct((b, h, HEAD_PAD, s), BF16),
                   jax.ShapeDtypeStruct((b, h, s, HEAD_PAD), BF16),
                   jax.ShapeDtypeStruct((b, h, ns, V_ROWS, tp), BF16)],
        compiler_params=_cparams("parallel", "parallel"),
        name="mla_prep",
    )(pos, proj, proj, proj, g_cq.reshape(1, -1), w_uq.T.astype(BF16), g_ckv.reshape(1, -1), w_ukv.T.astype(BF16),
      g_qk_q.reshape(-1, 1), g_qk_k.reshape(-1, 1), inv_freq)


def _attn_body(qT_ref, k_ref, vT_ref, o_ref, *, tq, nh):
    i = pl.program_id(2)

    def step(j, carry, masked):
        scores = []
        for h in range(nh):
            kb = k_ref[0, h, pl.ds(pl.multiple_of(j * tq, tq), tq), :]
            s = jnp.dot(kb, qT_ref[0, h], preferred_element_type=F32)
            if masked:
                kpos = lax.broadcasted_iota(jnp.int32, (tq, tq), 0)
                qpos = lax.broadcasted_iota(jnp.int32, (tq, tq), 1)
                s = jnp.where(kpos <= qpos, s, -jnp.inf)
            scores.append(s)
        out = []
        for h in range(nh):
            m, acc = carry[h]
            m_new = jnp.maximum(m, jnp.max(scores[h], axis=0, keepdims=True))
            p = jnp.exp2(scores[h] - m_new).astype(BF16)
            acc = jnp.exp2(m - m_new) * acc + jnp.dot(vT_ref[0, h, j], p, preferred_element_type=F32)
            out.append((m_new, acc))
        return tuple(out)

    carry = tuple((jnp.full((1, tq), -jnp.inf, F32), jnp.zeros((V_ROWS, tq), F32)) for _ in range(nh))
    carry = lax.fori_loop(0, i, functools.partial(step, masked=False), carry)
    carry = step(i, carry, True)
    for h in range(nh):
        acc = carry[h][1]
        o_ref[0, h] = (acc[:MLA_V] / acc[MLA_V:MLA_V + 1]).astype(o_ref.dtype)


def _attention(qT, k, vT):
    b, h, _, s = qT.shape
    tq = vT.shape[-1]
    nk = s // tq
    nh = ATT_NH
    return pl.pallas_call(
        functools.partial(_attn_body, tq=tq, nh=nh),
        grid=(b, h // nh, s // tq),
        in_specs=[pl.BlockSpec((1, nh, HEAD_PAD, tq), lambda bi, hi, qi: (bi, hi, 0, qi)),
                  pl.BlockSpec((1, nh, s, HEAD_PAD), lambda bi, hi, qi: (bi, hi, 0, 0)),
                  pl.BlockSpec((1, nh, nk, V_ROWS, tq), lambda bi, hi, qi: (bi, hi, 0, 0, 0))],
        out_specs=pl.BlockSpec((1, nh, MLA_V, tq), lambda bi, hi, qi: (bi, hi, 0, qi)),
        out_shape=jax.ShapeDtypeStruct((b, h, MLA_V, s), BF16),
        compiler_params=_cparams("parallel", "parallel", "arbitrary"),
        name="mla_attention",
    )(qT, k, vT)


def _gla_body(q_ref, k_ref, v_ref, og_ref, small_ref, wg_ref, bg_ref, gout_ref, o_ref, st_ref, *, blk):
    @pl.when(pl.program_id(1) == 0)
    def _():
        st_ref[...] = jnp.zeros_like(st_ref)

    hd = GLA_HEADS * GLA_DK
    nb = q_ref.shape[0]
    row = lax.broadcasted_iota(jnp.int32, (blk, blk), 0)
    col = lax.broadcasted_iota(jnp.int32, (blk, blk), 1)
    tril = (col <= row).astype(F32)

    def prepare(bb):
        lr = small_ref[bb][:, SMALL_LR:SMALL_LR + GLA_GATE_RANK].astype(F32)
        z = jnp.dot(lr, wg_ref[...], precision=HIGHEST, preferred_element_type=F32) + bg_ref[...]
        log_a = jax.nn.log_sigmoid(z) * (1.0 / GLA_TAU)
        bc = jnp.dot(tril, log_a, precision=HIGHEST, preferred_element_type=F32)
        q = q_ref[bb].astype(F32) * (GLA_DK ** -0.5)
        k = k_ref[bb].astype(F32)

        def ref_rows(group, off):
            parts = [jnp.broadcast_to(bc[g0 + off:g0 + off + 1], (group, hd)) for g0 in range(0, blk, group)]
            return parts[0] if len(parts) == 1 else jnp.concatenate(parts, axis=0)

        levels = []
        group = blk
        while group > GLA_BASE:
            half = group // 2
            ref = ref_rows(group, half - 1)
            qs = (q * jnp.exp(jnp.minimum(bc - ref, 0.0))).astype(BF16)
            ks = (k * jnp.exp(jnp.minimum(ref - bc, 0.0))).astype(BF16)
            mask = ((row & -group) == (col & -group)) & ((row & half) != 0) & ((col & half) == 0)
            levels.append((qs, ks, mask))
            group = half
        if GLA_BASE == 1:
            levels.append((q.astype(BF16), k.astype(BF16), row == col))
        else:
            ref = ref_rows(GLA_BASE, GLA_BASE // 2 - 1)
            mask = ((row & -GLA_BASE) == (col & -GLA_BASE)) & (col <= row)
            levels.append(((q * jnp.exp(bc - ref)).astype(BF16), (k * jnp.exp(ref - bc)).astype(BF16), mask))
        b_last = bc[blk - 1:blk]
        return dict(levels=levels, qd=(q * jnp.exp(bc)).astype(BF16), kd=(k * jnp.exp(b_last - bc)).astype(BF16),
                    a_last=jnp.exp(b_last))

    prep = [prepare(bb) for bb in range(nb)]
    gout = gout_ref[...]
    for h in range(GLA_HEADS):
        ks_, vs_ = slice(h * GLA_DK, (h + 1) * GLA_DK), slice(h * GLA_DV, (h + 1) * GLA_DV)
        for bb in range(nb):
            pr = prep[bb]
            attn = jnp.zeros((blk, blk), F32)
            for qs, ksc, mask in pr["levels"]:
                p = lax.dot_general(qs[:, ks_], ksc[:, ks_], _NT, preferred_element_type=F32)
                attn = jnp.where(mask, p, attn)
            v_h = v_ref[bb, :, vs_]
            st = st_ref[bb, h]
            o = jnp.dot(attn.astype(BF16), v_h, preferred_element_type=F32)
            o = o + lax.dot_general(pr["qd"][:, ks_], st.astype(BF16), _NT, preferred_element_type=F32)
            st_ref[bb, h] = (st * pr["a_last"][:, ks_]
                             + lax.dot_general(v_h, pr["kd"][:, ks_], _TN, preferred_element_type=F32))
            og = og_ref[bb, :, vs_].astype(F32)
            o_ref[bb, :, vs_] = (_rms(o) * gout * (og * jax.nn.sigmoid(og))).astype(o_ref.dtype)


def _gla(proj, w_gate_up, b_gate, g_out):
    b, s, _ = proj.shape
    blk = _tile(s, GLA_L)
    nb = _tile(b, GLA_NB)
    hd, hv = GLA_HEADS * GLA_DK, GLA_HEADS * GLA_DV
    const = lambda shape: pl.BlockSpec(shape, lambda bi, si: (0,) * len(shape))
    return pl.pallas_call(
        functools.partial(_gla_body, blk=blk),
        grid=(b // nb, s // blk),
        in_specs=[pl.BlockSpec((nb, blk, hd), lambda bi, si: (bi, si, COL_GQ // hd)),
                  pl.BlockSpec((nb, blk, hd), lambda bi, si: (bi, si, COL_GK // hd)),
                  pl.BlockSpec((nb, blk, hv), lambda bi, si: (bi, si, COL_GV // hv)),
                  pl.BlockSpec((nb, blk, hv), lambda bi, si: (bi, si, COL_GOG // hv)),
                  pl.BlockSpec((nb, blk, LANES), lambda bi, si: (bi, si, COL_SMALL // LANES)),
                  const((GLA_GATE_RANK, hd)), const((1, hd)), const((1, GLA_DV))],
        out_specs=pl.BlockSpec((nb, blk, hv), lambda bi, si: (bi, si, 0)),
        out_shape=jax.ShapeDtypeStruct((b, s, hv), BF16),
        scratch_shapes=[pltpu.VMEM((nb, GLA_HEADS, GLA_DV, GLA_DK), F32)],
        compiler_params=_cparams("parallel", "arbitrary"),
        name="gla",
    )(proj, proj, proj, proj, proj, w_gate_up, b_gate.reshape(1, -1), g_out.reshape(1, -1))


def _pool_body(u_ref, halo_ref, w_ref, scale_ref, o_ref, *, tm):
    si = pl.program_id(1)
    row = lax.broadcasted_iota(jnp.int32, (tm, tm + POOL_HALO), 0)
    col = lax.broadcasted_iota(jnp.int32, (tm, tm + POOL_HALO), 1)
    t1 = (si * tm + lax.broadcasted_iota(jnp.int32, (tm, 1), 0) + 1).astype(F32)
    for g, win in enumerate(POOL_WINDOWS):
        cs = slice(g * POOL_GROUP, (g + 1) * POOL_GROUP)
        cur = u_ref[0, :, cs]
        halo = halo_ref[0, :, cs]
        ext = jnp.concatenate([jnp.where(si > 0, halo, jnp.zeros_like(halo)), cur], axis=0)
        band = ((col <= row + POOL_HALO) & (col > row + POOL_HALO - win)).astype(BF16)
        wsum = jnp.dot(band, ext, preferred_element_type=F32)
        pooled = wsum / jnp.minimum(t1, float(win)) - cur.astype(F32)
        y = jnp.dot(pooled.astype(BF16), w_ref[g], preferred_element_type=F32)
        o_ref[0, :, cs] = (y * scale_ref[:, cs]).astype(o_ref.dtype)


def _pool(proj, w_pool, pool_scale):
    b, s, _ = proj.shape
    tm = _tile(s, TM_POOL)
    hb = tm // POOL_HALO
    return pl.pallas_call(
        functools.partial(_pool_body, tm=tm),
        grid=(b, s // tm),
        in_specs=[pl.BlockSpec((1, tm, POOL_WIDTH), lambda bi, si: (bi, si, COL_POOL // POOL_WIDTH)),
                  pl.BlockSpec((1, POOL_HALO, POOL_WIDTH),
                               lambda bi, si: (bi, jnp.maximum(si * hb - 1, 0), COL_POOL // POOL_WIDTH)),
                  pl.BlockSpec((len(POOL_WINDOWS), POOL_GROUP, POOL_GROUP), lambda bi, si: (0, 0, 0)),
                  pl.BlockSpec((1, POOL_WIDTH), lambda bi, si: (0, 0))],
        out_specs=pl.BlockSpec((1, tm, POOL_WIDTH), lambda bi, si: (bi, si, 0)),
        out_shape=jax.ShapeDtypeStruct((b, s, POOL_WIDTH), BF16),
        compiler_params=_cparams("parallel", "parallel"),
        name="pool",
    )(proj, proj, w_pool.astype(BF16), pool_scale.reshape(1, -1))


def _merge_body(x_ref, yaT_ref, yb_ref, yc_ref, g0_ref, g1_ref, g2_ref, wb_ref, wo_ref, o_ref):
    d_a = lax.dot_general(yaT_ref[0], wb_ref[0], _TN, preferred_element_type=F32)
    d_b = jnp.dot(yb_ref[0], wb_ref[1], preferred_element_type=F32)
    d_c = jnp.dot(yc_ref[0], wb_ref[2], preferred_element_type=F32)
    merged = (jax.nn.sigmoid(g0_ref[0].astype(F32)) * d_a + jax.nn.sigmoid(g1_ref[0].astype(F32)) * d_b
              + jax.nn.sigmoid(g2_ref[0].astype(F32)) * d_c)
    o_ref[0] = x_ref[0] + jnp.dot(merged.astype(BF16), wo_ref[...], preferred_element_type=F32)


def _merge(x, yaT, yb, yc, proj, w_branch, w_out):
    b, s, d = x.shape
    tm = _tile(s, TM_MERGE)
    tok = lambda width, cb: pl.BlockSpec((1, tm, width), lambda bi, si: (bi, si, cb))
    return pl.pallas_call(
        _merge_body,
        grid=(b, s // tm),
        in_specs=[tok(d, 0),
                  pl.BlockSpec((1, BRANCH_WIDTH, tm), lambda bi, si: (bi, 0, si)),
                  tok(BRANCH_WIDTH, 0), tok(BRANCH_WIDTH, 0),
                  tok(d, COL_GATES // d), tok(d, COL_GATES // d + 1), tok(d, COL_GATES // d + 2),
                  pl.BlockSpec((N_BRANCH, BRANCH_WIDTH, d), lambda bi, si: (0, 0, 0)),
                  pl.BlockSpec((d, d), lambda bi, si: (0, 0))],
        out_specs=tok(d, 0),
        out_shape=jax.ShapeDtypeStruct((b, s, d), F32),
        compiler_params=_cparams("parallel", "parallel"),
        name="merge_out_proj",
    )(x, yaT, yb, yc, proj, proj, proj, w_branch.astype(BF16), w_out.astype(BF16))


def _ffn_body(x_ref, g_ref, wg_ref, wu_ref, wd_ref, o_ref, act_ref, *, tf):
    h = (_rms(x_ref[...]) * g_ref[...]).astype(BF16)
    nf = act_ref.shape[0]
    for c in range(nf):
        a = jnp.dot(h, wg_ref[:, c * tf:(c + 1) * tf], preferred_element_type=F32)
        u = jnp.dot(h, wu_ref[:, c * tf:(c + 1) * tf], preferred_element_type=F32)
        act_ref[c] = (a * jax.nn.sigmoid(a) * u).astype(BF16)
    act = jnp.concatenate([act_ref[c] for c in range(nf)], axis=1)
    o_ref[...] = x_ref[...] + jnp.dot(act, wd_ref[...], preferred_element_type=F32)


def _ffn(x2d, g, w_gate, w_up, w_down):
    t, d = x2d.shape
    ff = w_gate.shape[1]
    tm, tf = _tile(t, TM_FFN), _tile(ff, TF_FFN)
    resident = lambda shape: pl.BlockSpec(shape, lambda i: (0, 0), pipeline_mode=pl.Buffered(1))
    return pl.pallas_call(
        functools.partial(_ffn_body, tf=tf),
        grid=(t // tm,),
        in_specs=[pl.BlockSpec((tm, d), lambda i: (i, 0)),
                  pl.BlockSpec((1, d), lambda i: (0, 0)),
                  resident((d, ff)), resident((d, ff)), resident((ff, d))],
        out_specs=pl.BlockSpec((tm, d), lambda i: (i, 0)),
        out_shape=jax.ShapeDtypeStruct((t, d), F32),
        scratch_shapes=[pltpu.VMEM((ff // tf, tm, tf), BF16)],
        compiler_params=_cparams("parallel"),
        name="ffn_dense",
    )(x2d, g.reshape(1, d), w_gate.astype(BF16), w_up.astype(BF16), w_down.astype(BF16))


ROW_TILE = (8, LANES)


def _store_row_tiles(ref, val):
    for c in range(ROW_TILE[0]):
        ref[:, c, :] = val[:, c * LANES:(c + 1) * LANES]


def _load_row_tiles(ref_view):
    return jnp.concatenate([ref_view[:, c, :] for c in range(ROW_TILE[0])], axis=1)


def _route_body(x_ref, g_ref, wr_ref, h_ref, idx_ref, w_ref):
    hf = _rms(x_ref[...]) * g_ref[...]
    _store_row_tiles(h_ref, hf)
    logits = jnp.dot(hf, wr_ref[...], precision=HIGHEST, preferred_element_type=F32)
    lane = lax.broadcasted_iota(jnp.int32, logits.shape, 1)
    m1 = jnp.max(logits, axis=-1, keepdims=True)
    i1 = jnp.min(jnp.where(logits == m1, lane, N_EXPERTS), axis=-1, keepdims=True)
    rest = jnp.where(lane == i1, -jnp.inf, logits)
    m2 = jnp.max(rest, axis=-1, keepdims=True)
    i2 = jnp.min(jnp.where(rest == m2, lane, N_EXPERTS), axis=-1, keepdims=True)
    e2 = jnp.exp(m2 - m1)
    w1 = 1.0 / (1.0 + e2)
    first = lax.broadcasted_iota(jnp.int32, idx_ref.shape, 1) == 0
    idx_ref[...] = jnp.where(first, i1, i2)
    w_ref[...] = jnp.where(first, w1, e2 * w1)


def _moe_route(x2d, g, w_router):
    t, d = x2d.shape
    ne = w_router.shape[1]
    tm = _tile(t, TM_ROUTE)
    return pl.pallas_call(
        _route_body,
        grid=(t // tm,),
        in_specs=[pl.BlockSpec((tm, d), lambda i: (i, 0)),
                  pl.BlockSpec((1, d), lambda i: (0, 0)),
                  pl.BlockSpec((d, ne), lambda i: (0, 0))],
        out_specs=[pl.BlockSpec((tm,) + ROW_TILE, lambda i: (i, 0, 0)),
                   pl.BlockSpec((tm, TOP_K), lambda i: (i, 0)),
                   pl.BlockSpec((tm, TOP_K), lambda i: (i, 0))],
        out_shape=[jax.ShapeDtypeStruct((t,) + ROW_TILE, F32),
                   jax.ShapeDtypeStruct((t, TOP_K), jnp.int32),
                   jax.ShapeDtypeStruct((t, TOP_K), F32)],
        compiler_params=_cparams("parallel"),
        name="moe_route",
    )(x2d, g.reshape(1, d), w_router)


def _moe_plan(idx, tm):
    t = idx.shape[0]
    e = idx.reshape(-1)
    onehot = (e[:, None] == jnp.arange(N_EXPERTS, dtype=jnp.int32)[None, :]).astype(jnp.int32)
    csum = jnp.cumsum(onehot, axis=0)
    rank = jnp.sum((csum - onehot) * onehot, axis=1)
    tiles = (csum[-1] + tm - 1) // tm
    tile_end = jnp.cumsum(tiles)
    row0 = (tile_end - tiles) * tm
    dest = jnp.sum(onehot * row0[None, :], axis=1) + rank
    n_tiles = (TOP_K * t) // tm + N_EXPERTS
    row_token = jnp.zeros((n_tiles * tm,), jnp.int32).at[dest].set(
        jnp.arange(TOP_K * t, dtype=jnp.int32) // TOP_K, unique_indices=True)
    tile_expert = jnp.sum(jnp.arange(n_tiles, dtype=jnp.int32)[:, None] >= tile_end[None, :], axis=1)
    tile_expert = jnp.minimum(tile_expert, N_EXPERTS - 1).astype(jnp.int32)
    return row_token.reshape(n_tiles, 1, tm), dest.reshape(t, TOP_K), tile_expert, tile_end[-1:].astype(jnp.int32)


def _gather_rows(idx_ref, src_hbm, buf, sem, lo, n):
    def body(p, carry):
        rows = [lo + GATHER_UNROLL * p + q for q in range(GATHER_UNROLL)]
        toks = [idx_ref[0, 0, r] for r in rows]
        for q, (r, tok) in enumerate(zip(rows, toks)):
            pltpu.make_async_copy(src_hbm.at[pl.ds(tok, 1)], buf.at[pl.ds(r, 1)], sem).start(priority=q % 2)
        return carry

    lax.fori_loop(0, n // GATHER_UNROLL, body, 0)


def _wait_rows(src_hbm, buf, sem, n):
    pltpu.make_async_copy(src_hbm.at[pl.ds(0, n)], buf, sem).wait()


def _moe_ffn_body(te_ref, nu_ref, cur_ref, nxt_ref, h_hbm, wg_ref, wu_ref, wd_ref, o_ref, xbuf, sem,
                  wgb, wub, wdb, *, tm):
    i = pl.program_id(0)
    slot = i % 2
    n_used = nu_ref[0]

    @pl.when(jnp.logical_and(i == 0, n_used > 0))
    def _():
        _gather_rows(cur_ref, h_hbm, xbuf.at[0], sem.at[0], 0, tm)

    @pl.when(i + 1 < n_used)
    def _():
        _gather_rows(nxt_ref, h_hbm, xbuf.at[1 - slot], sem.at[1 - slot], 0, tm)

    new_expert = jnp.logical_or(i == 0, te_ref[i] != te_ref[jnp.maximum(i - 1, 0)])

    @pl.when(jnp.logical_and(i < n_used, new_expert))
    def _():
        wgb[...] = wg_ref[0].astype(BF16)
        wub[...] = wu_ref[0].astype(BF16)
        wdb[...] = wd_ref[0].astype(BF16)

    @pl.when(i < n_used)
    def _():
        _wait_rows(h_hbm, xbuf.at[slot], sem.at[slot], tm)
        xs = _load_row_tiles(xbuf.at[slot]).astype(BF16)
        a = jnp.dot(xs, wgb[...], preferred_element_type=F32)
        u = jnp.dot(xs, wub[...], preferred_element_type=F32)
        act = (a * jax.nn.sigmoid(a) * u).astype(BF16)
        _store_row_tiles(o_ref, jnp.dot(act, wdb[...], preferred_element_type=F32))

    @pl.when(i >= n_used)
    def _():
        o_ref[...] = jnp.zeros_like(o_ref)


def _moe_ffn(h, row_token, tile_expert, n_used, w_gate, w_up, w_down):
    ne, d, fe = w_gate.shape
    assert h.shape[1:] == ROW_TILE and d == ROW_TILE[0] * ROW_TILE[1]
    n_tiles, _, tm = row_token.shape
    grid_spec = pltpu.PrefetchScalarGridSpec(
        num_scalar_prefetch=2,
        grid=(n_tiles,),
        in_specs=[pl.BlockSpec((1, 1, tm), lambda i, te, nu: (i, 0, 0), memory_space=pltpu.SMEM),
                  pl.BlockSpec((1, 1, tm), lambda i, te, nu: (jnp.minimum(i + 1, n_tiles - 1), 0, 0),
                               memory_space=pltpu.SMEM),
                  pl.BlockSpec(memory_space=pl.ANY),
                  pl.BlockSpec((1, d, fe), lambda i, te, nu: (te[i], 0, 0), pipeline_mode=pl.Buffered(1)),
                  pl.BlockSpec((1, d, fe), lambda i, te, nu: (te[i], 0, 0), pipeline_mode=pl.Buffered(1)),
                  pl.BlockSpec((1, fe, d), lambda i, te, nu: (te[i], 0, 0), pipeline_mode=pl.Buffered(1))],
        out_specs=pl.BlockSpec((tm,) + ROW_TILE, lambda i, te, nu: (i, 0, 0)),
        scratch_shapes=[pltpu.VMEM((2, tm) + ROW_TILE, F32), pltpu.SemaphoreType.DMA((2,)),
                        pltpu.VMEM((d, fe), BF16), pltpu.VMEM((d, fe), BF16), pltpu.VMEM((fe, d), BF16)])
    return pl.pallas_call(
        functools.partial(_moe_ffn_body, tm=tm),
        grid_spec=grid_spec,
        out_shape=jax.ShapeDtypeStruct((n_tiles * tm,) + ROW_TILE, F32),
        compiler_params=_cparams("arbitrary"),
        name="moe_ffn",
    )(tile_expert, n_used, row_token, row_token, h, w_gate, w_up, w_down)


def _moe_combine_body(cur_ref, nxt_ref, x_ref, w_ref, y_hbm, o_ref, ybuf, sem, *, tc):
    i = pl.program_id(0)
    slot = i % 2
    n = TOP_K * tc

    @pl.when(i == 0)
    def _():
        _gather_rows(cur_ref, y_hbm, ybuf.at[0], sem.at[0], 0, n)

    @pl.when(i + 1 < pl.num_programs(0))
    def _():
        _gather_rows(nxt_ref, y_hbm, ybuf.at[1 - slot], sem.at[1 - slot], 0, n)

    _wait_rows(y_hbm, ybuf.at[slot], sem.at[slot], n)
    w = w_ref[...]
    y0 = _load_row_tiles(ybuf.at[slot, pl.ds(0, tc)])
    y1 = _load_row_tiles(ybuf.at[slot, pl.ds(tc, tc)])
    o_ref[...] = x_ref[...] + w[:, 0:1] * y0 + w[:, 1:2] * y1


def _moe_combine(x2d, ys, dest, w):
    t, d = x2d.shape
    tc = _tile(t, TM_COMBINE)
    nt = t // tc
    dest_tiles = dest.reshape(nt, tc, TOP_K).transpose(0, 2, 1).reshape(nt, 1, TOP_K * tc)
    return pl.pallas_call(
        functools.partial(_moe_combine_body, tc=tc),
        grid=(nt,),
        in_specs=[pl.BlockSpec((1, 1, TOP_K * tc), lambda i: (i, 0, 0), memory_space=pltpu.SMEM),
                  pl.BlockSpec((1, 1, TOP_K * tc), lambda i: (jnp.minimum(i + 1, nt - 1), 0, 0),
                               memory_space=pltpu.SMEM),
                  pl.BlockSpec((tc, d), lambda i: (i, 0)),
                  pl.BlockSpec((tc, TOP_K), lambda i: (i, 0)),
                  pl.BlockSpec(memory_space=pl.ANY)],
        out_specs=pl.BlockSpec((tc, d), lambda i: (i, 0)),
        out_shape=jax.ShapeDtypeStruct((t, d), F32),
        scratch_shapes=[pltpu.VMEM((2, TOP_K * tc) + ROW_TILE, F32), pltpu.SemaphoreType.DMA((2,))],
        compiler_params=_cparams("arbitrary"),
        name="moe_combine",
    )(dest_tiles, dest_tiles, x2d, w, ys)


def _moe(x2d, g, w_router, w_gate, w_up, w_down):
    h, idx, w = _moe_route(x2d, g, w_router)
    row_token, dest, tile_expert, n_used = _moe_plan(idx, _tile(TOP_K * x2d.shape[0], TM_MOE))
    ys = _moe_ffn(h, row_token, tile_expert, n_used, w_gate, w_up, w_down)
    return _moe_combine(x2d, ys, dest, w)


def _permute_w_in(w):
    off = np.concatenate([[0], np.cumsum(IN_SPLITS)]).tolist()
    piece = lambda i: w[:, off[i]:off[i + 1]].astype(BF16)
    order = [9, 0, 1, 3, 4, 5, 7, 8, 2, 6]
    pad = jnp.zeros((w.shape[0], D_PROJ - off[-1]), BF16)
    return jnp.concatenate([piece(i) for i in order] + [pad], axis=1)


def kernel(x, positions, g_mix, w_in, g_cq, w_uq, g_ckv, w_ukv, g_qk_q, g_qk_k, w_gla_gate_up, b_gla_gate, g_gla_out, w_pool, pool_scale, w_branch, w_out, g_ffn, w_ffn_gate, w_ffn_up, w_ffn_down, w_router, w_exp_gate, w_exp_up, w_exp_down):
    b, s, d = x.shape
    depth = g_mix.shape[0]
    for layer in range(depth):
        proj = _norm_matmul(x.reshape(b * s, d), g_mix[layer], _permute_w_in(w_in[layer]), BF16).reshape(b, s, D_PROJ)
        qT, k, vT = _mla_prep(proj, positions, g_cq[layer], w_uq[layer], g_ckv[layer], w_ukv[layer],
                              g_qk_q[layer], g_qk_k[layer])
        yaT = _attention(qT, k, vT).reshape(b, MLA_HEADS * MLA_V, s)
        yb = _gla(proj, w_gla_gate_up[layer], b_gla_gate[layer], g_gla_out[layer])
        yc = _pool(proj, w_pool[layer], pool_scale[layer])
        x = _merge(x, yaT, yb, yc, proj, w_branch[layer], w_out[layer])
        i = layer // 2
        if layer % 2 == 0:
            x2d = _ffn(x.reshape(b * s, d), g_ffn[layer], w_ffn_gate[i], w_ffn_up[i], w_ffn_down[i])
        else:
            x2d = _moe(x.reshape(b * s, d), g_ffn[layer], w_router[i], w_exp_gate[i], w_exp_up[i], w_exp_down[i])
        x = x2d.reshape(b, s, d)
    return x
```

```python
import functools

import jax
import jax.numpy as jnp
import numpy as np
from jax import lax
from jax.experimental import pallas as pl
from jax.experimental.pallas import tpu as pltpu

F32 = jnp.float32
BF16 = jnp.bfloat16
HIGHEST = lax.Precision.HIGHEST

D_MODEL = 1024
MLA_HEADS = 8
MLA_NOPE = 64
MLA_ROPE = 32
MLA_QK = MLA_NOPE + MLA_ROPE
MLA_V = 64
MLA_Q_RANK = D_MODEL // 4
MLA_KV_RANK = D_MODEL // 4
ROPE_BASE = 10000.0
GLA_HEADS = 4
GLA_DK = 64
GLA_DV = 128
GLA_GATE_RANK = 16
GLA_TAU = 16.0
POOL_WINDOWS = (2, 4, 8, 16)
POOL_GROUP = 128
POOL_WIDTH = 4 * POOL_GROUP
N_BRANCH = 3
BRANCH_WIDTH = 512
D_FF = 2816
N_EXPERTS = 8
TOP_K = 2
D_EXPERT = 1408
EPS = 1e-6

IN_SPLITS = (MLA_Q_RANK, MLA_KV_RANK, MLA_ROPE,
             GLA_HEADS * GLA_DK, GLA_HEADS * GLA_DK, GLA_HEADS * GLA_DV, GLA_GATE_RANK, GLA_HEADS * GLA_DV,
             POOL_WIDTH, N_BRANCH * D_MODEL)

LANES = 128
HEAD_PAD = LANES

COL_GATES = 0
COL_CQ = 3072
COL_CKV = 3328
COL_GQ = 3584
COL_GK = 3840
COL_GV = 4096
COL_GOG = 4608
COL_POOL = 5120
COL_SMALL = 5632
D_PROJ = 5760
SMALL_LR = MLA_ROPE

TM_PROJ = 1024
TN_PROJ = 1536
ATT_T = 512
ATT_NH = 4
V_ROWS = 80
GLA_L = 128
GLA_BASE = 1
GLA_NB = 4
POOL_HALO = 16
TM_MERGE = 512
TM_FFN = 1024
TF_FFN = 256
TM_ROUTE = 1024
TM_MOE = 512
TM_COMBINE = 512
GATHER_UNROLL = 16
VMEM_LIMIT = 56 * 1024 * 1024


def _cparams(*sem):
    return pltpu.CompilerParams(dimension_semantics=sem, vmem_limit_bytes=VMEM_LIMIT)


def _tile(n, pref):
    t = min(n, pref)
    assert n % t == 0, (n, pref)
    return t


def _rms(x):
    return x * lax.rsqrt(jnp.mean(x * x, axis=-1, keepdims=True) + EPS)


_NT = (((1,), (1,)), ((), ()))
_TN = (((0,), (0,)), ((), ()))


def _norm_matmul_body(x_ref, g_ref, w_ref, o_ref):
    h = (_rms(x_ref[...]) * g_ref[...]).astype(BF16)
    n = o_ref.shape[1]
    for lo in range(0, n, TN_PROJ):
        hi = min(lo + TN_PROJ, n)
        o_ref[:, lo:hi] = jnp.dot(h, w_ref[:, lo:hi], preferred_element_type=F32).astype(o_ref.dtype)


def _norm_matmul(x2d, g, w, out_dtype):
    t, d = x2d.shape
    n = w.shape[1]
    tm = _tile(t, TM_PROJ)
    return pl.pallas_call(
        _norm_matmul_body,
        grid=(t // tm,),
        in_specs=[pl.BlockSpec((tm, d), lambda i: (i, 0)),
                  pl.BlockSpec((1, d), lambda i: (0, 0)),
                  pl.BlockSpec((d, n), lambda i: (0, 0), pipeline_mode=pl.Buffered(1))],
        out_specs=pl.BlockSpec((tm, n), lambda i: (i, 0)),
        out_shape=jax.ShapeDtypeStruct((t, n), out_dtype),
        compiler_params=_cparams("parallel"),
        name="norm_in_proj",
    )(x2d, g.reshape(1, d), w)


def _mla_prep_body(pos_ref, cq_ref, ckv_ref, small_ref, gcq_ref, wuq_ref, gckv_ref, wukv_ref, gq_ref, gk_ref,
                   invf_ref, qT_ref, k_ref, vT_ref):
    tp = cq_ref.shape[1]
    half = MLA_ROPE // 2
    ang = invf_ref[...] * pos_ref[0].astype(F32)
    cos, sin = jnp.cos(ang), jnp.sin(ang)

    def rope(t):
        x1, x2 = t[:half], t[half:]
        return x1 * cos - x2 * sin, x2 * cos + x1 * sin

    cqn = (_rms(cq_ref[0].astype(F32)) * gcq_ref[...]).astype(BF16)
    ckvn = (_rms(ckv_ref[0].astype(F32)) * gckv_ref[...]).astype(BF16)
    qT = lax.dot_general(wuq_ref[...], cqn, _NT, preferred_element_type=F32)
    kvT = lax.dot_general(wukv_ref[...], ckvn, _NT, preferred_element_type=F32)
    kr = small_ref[0].astype(F32).T[:MLA_ROPE]
    kr_ss = jnp.sum(kr * kr, axis=0, keepdims=True)
    gq, gk = gq_ref[...], gk_ref[...]
    zpad = jnp.zeros((HEAD_PAD - MLA_QK, tp), F32)
    vrow = lax.broadcasted_iota(jnp.int32, (V_ROWS - MLA_V, tp), 0)
    vpad = jnp.where(vrow == 0, 1.0, 0.0).astype(F32)
    scale = MLA_QK ** -0.5 * np.log2(np.e)
    for h in range(MLA_HEADS):
        qh = qT[h * MLA_QK:(h + 1) * MLA_QK]
        qn = qh * lax.rsqrt(jnp.mean(qh * qh, axis=0, keepdims=True) + EPS) * (gq * scale)
        r1, r2 = rope(qn[MLA_NOPE:])
        qT_ref[0, h] = jnp.concatenate([qn[:MLA_NOPE], r1, r2, zpad], axis=0).astype(BF16)
        base = h * (MLA_NOPE + MLA_V)
        kn = kvT[base:base + MLA_NOPE]
        r = lax.rsqrt((jnp.sum(kn * kn, axis=0, keepdims=True) + kr_ss) * (1.0 / MLA_QK) + EPS)
        r1, r2 = rope(kr * r * gk[MLA_NOPE:])
        kfm = jnp.concatenate([kn * r * gk[:MLA_NOPE], r1, r2, zpad], axis=0)
        k_ref[0, h] = kfm.T.astype(BF16)
        vT_ref[0, h, 0] = jnp.concatenate([kvT[base + MLA_NOPE:base + MLA_NOPE + MLA_V], vpad], axis=0).astype(BF16)


def _mla_prep(proj, positions, g_cq, w_uq, g_ckv, w_ukv, g_qk_q, g_qk_k):
    b, s, _ = proj.shape
    tp = _tile(s, ATT_T)
    ns = s // tp
    h = MLA_HEADS
    pos = positions.reshape(b * ns, 1, tp)
    inv_freq = (ROPE_BASE ** (-jnp.arange(MLA_ROPE // 2, dtype=F32) / (MLA_ROPE // 2))).reshape(-1, 1)
    const = lambda shape: pl.BlockSpec(shape, lambda bi, si: (0,) * len(shape))
    return pl.pallas_call(
        _mla_prep_body,
        grid=(b, ns),
        in_specs=[pl.BlockSpec((1, 1, tp), lambda bi, si: (bi * ns + si, 0, 0)),
                  pl.BlockSpec((1, tp, MLA_Q_RANK), lambda bi, si: (bi, si, COL_CQ // MLA_Q_RANK)),
                  pl.BlockSpec((1, tp, MLA_KV_RANK), lambda bi, si: (bi, si, COL_CKV // MLA_KV_RANK)),
                  pl.BlockSpec((1, tp, LANES), lambda bi, si: (bi, si, COL_SMALL // LANES)),
                  const((1, MLA_Q_RANK)), const((h * MLA_QK, MLA_Q_RANK)),
                  const((1, MLA_KV_RANK)), const((h * (MLA_NOPE + MLA_V), MLA_KV_RANK)),
                  const((MLA_QK, 1)), const((MLA_QK, 1)), const((MLA_ROPE // 2, 1))],
        out_specs=[pl.BlockSpec((1, h, HEAD_PAD, tp), lambda bi, si: (bi, 0, 0, si)),
                   pl.BlockSpec((1, h, tp, HEAD_PAD), lambda bi, si: (bi, 0, si, 0)),
                   pl.BlockSpec((1, h, 1, V_ROWS, tp), lambda bi, si: (bi, 0, si, 0, 0))],
        out_shape=[jax.ShapeDtypeStruct((b, h, HEAD_PAD, s), BF16),
                   jax.ShapeDtypeStruct((b, h, s, HEAD_PAD), BF16),
                   jax.ShapeDtypeStruct((b, h, ns, V_ROWS, tp), BF16)],
        compiler_params=_cparams("parallel", "parallel"),
        name="mla_prep",
    )(pos, proj, proj, proj, g_cq.reshape(1, -1), w_uq.T.astype(BF16), g_ckv.reshape(1, -1), w_ukv.T.astype(BF16),
      g_qk_q.reshape(-1, 1), g_qk_k.reshape(-1, 1), inv_freq)


def _attn_body(qT_ref, k_ref, vT_ref, o_ref, *, tq, nh):
    i = pl.program_id(2)

    def step(j, carry, masked):
        scores = []
        for h in range(nh):
            kb = k_ref[0, h, pl.ds(pl.multiple_of(j * tq, tq), tq), :]
            s = jnp.dot(kb, qT_ref[0, h], preferred_element_type=F32)
            if masked:
                kpos = lax.broadcasted_iota(jnp.int32, (tq, tq), 0)
                qpos = lax.broadcasted_iota(jnp.int32, (tq, tq), 1)
                s = jnp.where(kpos <= qpos, s, -jnp.inf)
            scores.append(s)
        out = []
        for h in range(nh):
            m, acc = carry[h]
            m_new = jnp.maximum(m, jnp.max(scores[h], axis=0, keepdims=True))
            p = jnp.exp2(scores[h] - m_new).astype(BF16)
            acc = jnp.exp2(m - m_new) * acc + jnp.dot(vT_ref[0, h, j], p, preferred_element_type=F32)
            out.append((m_new, acc))
        return tuple(out)

    carry = tuple((jnp.full((1, tq), -jnp.inf, F32), jnp.zeros((V_ROWS, tq), F32)) for _ in range(nh))
    carry = lax.fori_loop(0, i, functools.partial(step, masked=False), carry)
    carry = step(i, carry, True)
    for h in range(nh):
        acc = carry[h][1]
        o_ref[0, h] = (acc[:MLA_V] / acc[MLA_V:MLA_V + 1]).astype(o_ref.dtype)


def _attention(qT, k, vT):
    b, h, _, s = qT.shape
    tq = vT.shape[-1]
    nk = s // tq
    nh = ATT_NH
    return pl.pallas_call(
        functools.partial(_attn_body, tq=tq, nh=nh),
        grid=(b, h // nh, s // tq),
        in_specs=[pl.BlockSpec((1, nh, HEAD_PAD, tq), lambda bi, hi, qi: (bi, hi, 0, qi)),
                  pl.BlockSpec((1, nh, s, HEAD_PAD), lambda bi, hi, qi: (bi, hi, 0, 0)),
                  pl.BlockSpec((1, nh, nk, V_ROWS, tq), lambda bi, hi, qi: (bi, hi, 0, 0, 0))],
        out_specs=pl.BlockSpec((1, nh, MLA_V, tq), lambda bi, hi, qi: (bi, hi, 0, qi)),
        out_shape=jax.ShapeDtypeStruct((b, h, MLA_V, s), BF16),
        compiler_params=_cparams("parallel", "parallel", "arbitrary"),
        name="mla_attention",
    )(qT, k, vT)


def _gla_body(q_ref, k_ref, v_ref, og_ref, small_ref, wg_ref, bg_ref, gout_ref, o_ref, st_ref, *, blk):
    @pl.when(pl.program_id(1) == 0)
    def _():
        st_ref[...] = jnp.zeros_like(st_ref)

    hd = GLA_HEADS * GLA_DK
    nb = q_ref.shape[0]
    row = lax.broadcasted_iota(jnp.int32, (blk, blk), 0)
    col = lax.broadcasted_iota(jnp.int32, (blk, blk), 1)
    tril = (col <= row).astype(F32)

    def prepare(bb):
        lr = small_ref[bb][:, SMALL_LR:SMALL_LR + GLA_GATE_RANK].astype(F32)
        z = jnp.dot(lr, wg_ref[...], precision=HIGHEST, preferred_element_type=F32) + bg_ref[...]
        log_a = jax.nn.log_sigmoid(z) * (1.0 / GLA_TAU)
        bc = jnp.dot(tril, log_a, precision=HIGHEST, preferred_element_type=F32)
        q = q_ref[bb].astype(F32) * (GLA_DK ** -0.5)
        k = k_ref[bb].astype(F32)

        def ref_rows(group, off):
            parts = [jnp.broadcast_to(bc[g0 + off:g0 + off + 1], (group, hd)) for g0 in range(0, blk, group)]
            return parts[0] if len(parts) == 1 else jnp.concatenate(parts, axis=0)

        levels = []
        group = blk
        while group > GLA_BASE:
            half = group // 2
            ref = ref_rows(group, half - 1)
            qs = (q * jnp.exp(jnp.minimum(bc - ref, 0.0))).astype(BF16)
            ks = (k * jnp.exp(jnp.minimum(ref - bc, 0.0))).astype(BF16)
            mask = ((row & -group) == (col & -group)) & ((row & half) != 0) & ((col & half) == 0)
            levels.append((qs, ks, mask))
            group = half
        if GLA_BASE == 1:
            levels.append((q.astype(BF16), k.astype(BF16), row == col))
        else:
            ref = ref_rows(GLA_BASE, GLA_BASE // 2 - 1)
            mask = ((row & -GLA_BASE) == (col & -GLA_BASE)) & (col <= row)
            levels.append(((q * jnp.exp(bc - ref)).astype(BF16), (k * jnp.exp(ref - bc)).astype(BF16), mask))
        b_last = bc[blk - 1:blk]
        return dict(levels=levels, qd=(q * jnp.exp(bc)).astype(BF16), kd=(k * jnp.exp(b_last - bc)).astype(BF16),
                    a_last=jnp.exp(b_last))

    prep = [prepare(bb) for bb in range(nb)]
    gout = gout_ref[...]
    for h in range(GLA_HEADS):
        ks_, vs_ = slice(h * GLA_DK, (h + 1) * GLA_DK), slice(h * GLA_DV, (h + 1) * GLA_DV)
        for bb in range(nb):
            pr = prep[bb]
            attn = jnp.zeros((blk, blk), F32)
            for qs, ksc, mask in pr["levels"]:
                p = lax.dot_general(qs[:, ks_], ksc[:, ks_], _NT, preferred_element_type=F32)
                attn = jnp.where(mask, p, attn)
            v_h = v_ref[bb, :, vs_]
            st = st_ref[bb, h]
            o = jnp.dot(attn.astype(BF16), v_h, preferred_element_type=F32)
            o = o + lax.dot_general(pr["qd"][:, ks_], st.astype(BF16), _NT, preferred_element_type=F32)
            st_ref[bb, h] = (st * pr["a_last"][:, ks_]
                             + lax.dot_general(v_h, pr["kd"][:, ks_], _TN, preferred_element_type=F32))
            og = og_ref[bb, :, vs_].astype(F32)
            o_ref[bb, :, vs_] = (_rms(o) * gout * (og * jax.nn.sigmoid(og))).astype(o_ref.dtype)


def _gla(proj, w_gate_up, b_gate, g_out):
    b, s, _ = proj.shape
    blk = _tile(s, GLA_L)
    nb = _tile(b, GLA_NB)
    hd, hv = GLA_HEADS * GLA_DK, GLA_HEADS * GLA_DV
    const = lambda shape: pl.BlockSpec(shape, lambda bi, si: (0,) * len(shape))
    return pl.pallas_call(
        functools.partial(_gla_body, blk=blk),
        grid=(b // nb, s // blk),
        in_specs=[pl.BlockSpec((nb, blk, hd), lambda bi, si: (bi, si, COL_GQ // hd)),
                  pl.BlockSpec((nb, blk, hd), lambda bi, si: (bi, si, COL_GK // hd)),
                  pl.BlockSpec((nb, blk, hv), lambda bi, si: (bi, si, COL_GV // hv)),
                  pl.BlockSpec((nb, blk, hv), lambda bi, si: (bi, si, COL_GOG // hv)),
                  pl.BlockSpec((nb, blk, LANES), lambda bi, si: (bi, si, COL_SMALL // LANES)),
                  const((GLA_GATE_RANK, hd)), const((1, hd)), const((1, GLA_DV))],
        out_specs=pl.BlockSpec((nb, blk, hv), lambda bi, si: (bi, si, 0)),
        out_shape=jax.ShapeDtypeStruct((b, s, hv), BF16),
        scratch_shapes=[pltpu.VMEM((nb, GLA_HEADS, GLA_DV, GLA_DK), F32)],
        compiler_params=_cparams("parallel", "arbitrary"),
        name="gla",
    )(proj, proj, proj, proj, proj, w_gate_up, b_gate.reshape(1, -1), g_out.reshape(1, -1))


def _pool_tile(u_ref, halo_ref, w_ref, scale_ref, si):
    tm = u_ref.shape[1]
    row = lax.broadcasted_iota(jnp.int32, (tm, tm + POOL_HALO), 0)
    col = lax.broadcasted_iota(jnp.int32, (tm, tm + POOL_HALO), 1)
    t1 = (si * tm + lax.broadcasted_iota(jnp.int32, (tm, 1), 0) + 1).astype(F32)
    out = []
    for g, win in enumerate(POOL_WINDOWS):
        cs = slice(g * POOL_GROUP, (g + 1) * POOL_GROUP)
        cur = u_ref[0, :, cs]
        halo = halo_ref[0, :, cs]
        ext = jnp.concatenate([jnp.where(si > 0, halo, jnp.zeros_like(halo)), cur], axis=0)
        band = ((col <= row + POOL_HALO) & (col > row + POOL_HALO - win)).astype(BF16)
        wsum = jnp.dot(band, ext, preferred_element_type=F32)
        pooled = wsum / jnp.minimum(t1, float(win)) - cur.astype(F32)
        y = jnp.dot(pooled.astype(BF16), w_ref[g], preferred_element_type=F32)
        out.append((y * scale_ref[:, cs]).astype(BF16))
    return jnp.concatenate(out, axis=1)


def _merge_body(x_ref, yaT_ref, yb_ref, u_ref, halo_ref, g0_ref, g1_ref, g2_ref, wp_ref, ps_ref, wb_ref, wo_ref,
                o_ref):
    yc = _pool_tile(u_ref, halo_ref, wp_ref, ps_ref, pl.program_id(1))
    d_a = lax.dot_general(yaT_ref[0], wb_ref[0], _TN, preferred_element_type=F32)
    d_b = jnp.dot(yb_ref[0], wb_ref[1], preferred_element_type=F32)
    d_c = jnp.dot(yc, wb_ref[2], preferred_element_type=F32)
    merged = (jax.nn.sigmoid(g0_ref[0].astype(F32)) * d_a + jax.nn.sigmoid(g1_ref[0].astype(F32)) * d_b
              + jax.nn.sigmoid(g2_ref[0].astype(F32)) * d_c)
    o_ref[0] = x_ref[0] + jnp.dot(merged.astype(BF16), wo_ref[...], preferred_element_type=F32)


def _merge(x, yaT, yb, proj, w_pool, pool_scale, w_branch, w_out):
    b, s, d = x.shape
    tm = _tile(s, TM_MERGE)
    hb = tm // POOL_HALO
    tok = lambda width, cb: pl.BlockSpec((1, tm, width), lambda bi, si: (bi, si, cb))
    const = lambda shape: pl.BlockSpec(shape, lambda bi, si: (0,) * len(shape))
    return pl.pallas_call(
        _merge_body,
        grid=(b, s // tm),
        in_specs=[tok(d, 0),
                  pl.BlockSpec((1, BRANCH_WIDTH, tm), lambda bi, si: (bi, 0, si)),
                  tok(BRANCH_WIDTH, 0),
                  tok(POOL_WIDTH, COL_POOL // POOL_WIDTH),
                  pl.BlockSpec((1, POOL_HALO, POOL_WIDTH),
                               lambda bi, si: (bi, jnp.maximum(si * hb - 1, 0), COL_POOL // POOL_WIDTH)),
                  tok(d, COL_GATES // d), tok(d, COL_GATES // d + 1), tok(d, COL_GATES // d + 2),
                  const((len(POOL_WINDOWS), POOL_GROUP, POOL_GROUP)), const((1, POOL_WIDTH)),
                  const((N_BRANCH, BRANCH_WIDTH, d)), const((d, d))],
        out_specs=tok(d, 0),
        out_shape=jax.ShapeDtypeStruct((b, s, d), F32),
        compiler_params=_cparams("parallel", "parallel"),
        name="merge_out_proj",
    )(x, yaT, yb, proj, proj, proj, proj, proj, w_pool.astype(BF16), pool_scale.reshape(1, -1),
      w_branch.astype(BF16), w_out.astype(BF16))


def _ffn_body(x_ref, g_ref, wg_ref, wu_ref, wd_ref, o_ref, act_ref, *, tf):
    h = (_rms(x_ref[...]) * g_ref[...]).astype(BF16)
    nf = act_ref.shape[0]
    for c in range(nf):
        a = jnp.dot(h, wg_ref[:, c * tf:(c + 1) * tf], preferred_element_type=F32)
        u = jnp.dot(h, wu_ref[:, c * tf:(c + 1) * tf], preferred_element_type=F32)
        act_ref[c] = (a * jax.nn.sigmoid(a) * u).astype(BF16)
    act = jnp.concatenate([act_ref[c] for c in range(nf)], axis=1)
    o_ref[...] = x_ref[...] + jnp.dot(act, wd_ref[...], preferred_element_type=F32)


def _ffn(x2d, g, w_gate, w_up, w_down):
    t, d = x2d.shape
    ff = w_gate.shape[1]
    tm, tf = _tile(t, TM_FFN), _tile(ff, TF_FFN)
    resident = lambda shape: pl.BlockSpec(shape, lambda i: (0, 0), pipeline_mode=pl.Buffered(1))
    return pl.pallas_call(
        functools.partial(_ffn_body, tf=tf),
        grid=(t // tm,),
        in_specs=[pl.BlockSpec((tm, d), lambda i: (i, 0)),
                  pl.BlockSpec((1, d), lambda i: (0, 0)),
                  resident((d, ff)), resident((d, ff)), resident((ff, d))],
        out_specs=pl.BlockSpec((tm, d), lambda i: (i, 0)),
        out_shape=jax.ShapeDtypeStruct((t, d), F32),
        scratch_shapes=[pltpu.VMEM((ff // tf, tm, tf), BF16)],
        compiler_params=_cparams("parallel"),
        name="ffn_dense",
    )(x2d, g.reshape(1, d), w_gate.astype(BF16), w_up.astype(BF16), w_down.astype(BF16))


ROW_TILE = (8, LANES)


def _store_row_tiles(ref, val):
    for c in range(ROW_TILE[0]):
        ref[:, c, :] = val[:, c * LANES:(c + 1) * LANES]


def _load_row_tiles(ref_view):
    return jnp.concatenate([ref_view[:, c, :] for c in range(ROW_TILE[0])], axis=1)


def _route_body(x_ref, g_ref, wr_ref, h_ref, idx_ref, w_ref):
    hf = _rms(x_ref[...]) * g_ref[...]
    _store_row_tiles(h_ref, hf)
    logits = jnp.dot(hf, wr_ref[...], precision=HIGHEST, preferred_element_type=F32)
    lane = lax.broadcasted_iota(jnp.int32, logits.shape, 1)
    m1 = jnp.max(logits, axis=-1, keepdims=True)
    i1 = jnp.min(jnp.where(logits == m1, lane, N_EXPERTS), axis=-1, keepdims=True)
    rest = jnp.where(lane == i1, -jnp.inf, logits)
    m2 = jnp.max(rest, axis=-1, keepdims=True)
    i2 = jnp.min(jnp.where(rest == m2, lane, N_EXPERTS), axis=-1, keepdims=True)
    e2 = jnp.exp(m2 - m1)
    w1 = 1.0 / (1.0 + e2)
    first = lax.broadcasted_iota(jnp.int32, idx_ref.shape, 1) == 0
    idx_ref[...] = jnp.where(first, i1, i2)
    w_ref[...] = jnp.where(first, w1, e2 * w1)


def _moe_route(x2d, g, w_router):
    t, d = x2d.shape
    ne = w_router.shape[1]
    tm = _tile(t, TM_ROUTE)
    return pl.pallas_call(
        _route_body,
        grid=(t // tm,),
        in_specs=[pl.BlockSpec((tm, d), lambda i: (i, 0)),
                  pl.BlockSpec((1, d), lambda i: (0, 0)),
                  pl.BlockSpec((d, ne), lambda i: (0, 0))],
        out_specs=[pl.BlockSpec((tm,) + ROW_TILE, lambda i: (i, 0, 0)),
                   pl.BlockSpec((tm, TOP_K), lambda i: (i, 0)),
                   pl.BlockSpec((tm, TOP_K), lambda i: (i, 0))],
        out_shape=[jax.ShapeDtypeStruct((t,) + ROW_TILE, F32),
                   jax.ShapeDtypeStruct((t, TOP_K), jnp.int32),
                   jax.ShapeDtypeStruct((t, TOP_K), F32)],
        compiler_params=_cparams("parallel"),
        name="moe_route",
    )(x2d, g.reshape(1, d), w_router)


def _moe_plan(idx, tm):
    t = idx.shape[0]
    e = idx.reshape(-1)
    onehot = (e[:, None] == jnp.arange(N_EXPERTS, dtype=jnp.int32)[None, :]).astype(jnp.int32)
    csum = jnp.cumsum(onehot, axis=0)
    rank = jnp.sum((csum - onehot) * onehot, axis=1)
    tiles = (csum[-1] + tm - 1) // tm
    tile_end = jnp.cumsum(tiles)
    row0 = (tile_end - tiles) * tm
    dest = jnp.sum(onehot * row0[None, :], axis=1) + rank
    n_tiles = (TOP_K * t) // tm + N_EXPERTS
    row_token = jnp.zeros((n_tiles * tm,), jnp.int32).at[dest].set(
        jnp.arange(TOP_K * t, dtype=jnp.int32) // TOP_K, unique_indices=True)
    tile_expert = jnp.sum(jnp.arange(n_tiles, dtype=jnp.int32)[:, None] >= tile_end[None, :], axis=1)
    tile_expert = jnp.minimum(tile_expert, N_EXPERTS - 1).astype(jnp.int32)
    return row_token.reshape(n_tiles, 1, tm), dest.reshape(t, TOP_K), tile_expert, tile_end[-1:].astype(jnp.int32)


def _gather_rows(idx_ref, src_hbm, buf, sem, lo, n):
    def body(p, carry):
        rows = [lo + GATHER_UNROLL * p + q for q in range(GATHER_UNROLL)]
        toks = [idx_ref[0, 0, r] for r in rows]
        for q, (r, tok) in enumerate(zip(rows, toks)):
            pltpu.make_async_copy(src_hbm.at[pl.ds(tok, 1)], buf.at[pl.ds(r, 1)], sem).start(priority=q % 2)
        return carry

    lax.fori_loop(0, n // GATHER_UNROLL, body, 0)


def _wait_rows(src_hbm, buf, sem, n):
    pltpu.make_async_copy(src_hbm.at[pl.ds(0, n)], buf, sem).wait()


def _moe_ffn_body(te_ref, nu_ref, cur_ref, nxt_ref, h_hbm, wg_ref, wu_ref, wd_ref, o_ref, xbuf, sem,
                  wgb, wub, wdb, *, tm):
    i = pl.program_id(0)
    slot = i % 2
    n_used = nu_ref[0]

    @pl.when(jnp.logical_and(i == 0, n_used > 0))
    def _():
        _gather_rows(cur_ref, h_hbm, xbuf.at[0], sem.at[0], 0, tm)

    @pl.when(i + 1 < n_used)
    def _():
        _gather_rows(nxt_ref, h_hbm, xbuf.at[1 - slot], sem.at[1 - slot], 0, tm)

    new_expert = jnp.logical_or(i == 0, te_ref[i] != te_ref[jnp.maximum(i - 1, 0)])

    @pl.when(jnp.logical_and(i < n_used, new_expert))
    def _():
        wgb[...] = wg_ref[0].astype(BF16)
        wub[...] = wu_ref[0].astype(BF16)
        wdb[...] = wd_ref[0].astype(BF16)

    @pl.when(i < n_used)
    def _():
        _wait_rows(h_hbm, xbuf.at[slot], sem.at[slot], tm)
        xs = _load_row_tiles(xbuf.at[slot]).astype(BF16)
        a = jnp.dot(xs, wgb[...], preferred_element_type=F32)
        u = jnp.dot(xs, wub[...], preferred_element_type=F32)
        act = (a * jax.nn.sigmoid(a) * u).astype(BF16)
        _store_row_tiles(o_ref, jnp.dot(act, wdb[...], preferred_element_type=F32))

    @pl.when(i >= n_used)
    def _():
        o_ref[...] = jnp.zeros_like(o_ref)


def _moe_ffn(h, row_token, tile_expert, n_used, w_gate, w_up, w_down):
    ne, d, fe = w_gate.shape
    assert h.shape[1:] == ROW_TILE and d == ROW_TILE[0] * ROW_TILE[1]
    n_tiles, _, tm = row_token.shape
    grid_spec = pltpu.PrefetchScalarGridSpec(
        num_scalar_prefetch=2,
        grid=(n_tiles,),
        in_specs=[pl.BlockSpec((1, 1, tm), lambda i, te, nu: (i, 0, 0), memory_space=pltpu.SMEM),
                  pl.BlockSpec((1, 1, tm), lambda i, te, nu: (jnp.minimum(i + 1, n_tiles - 1), 0, 0),
                               memory_space=pltpu.SMEM),
                  pl.BlockSpec(memory_space=pl.ANY),
                  pl.BlockSpec((1, d, fe), lambda i, te, nu: (te[i], 0, 0), pipeline_mode=pl.Buffered(1)),
                  pl.BlockSpec((1, d, fe), lambda i, te, nu: (te[i], 0, 0), pipeline_mode=pl.Buffered(1)),
                  pl.BlockSpec((1, fe, d), lambda i, te, nu: (te[i], 0, 0), pipeline_mode=pl.Buffered(1))],
        out_specs=pl.BlockSpec((tm,) + ROW_TILE, lambda i, te, nu: (i, 0, 0)),
        scratch_shapes=[pltpu.VMEM((2, tm) + ROW_TILE, F32), pltpu.SemaphoreType.DMA((2,)),
                        pltpu.VMEM((d, fe), BF16), pltpu.VMEM((d, fe), BF16), pltpu.VMEM((fe, d), BF16)])
    return pl.pallas_call(
        functools.partial(_moe_ffn_body, tm=tm),
        grid_spec=grid_spec,
        out_shape=jax.ShapeDtypeStruct((n_tiles * tm,) + ROW_TILE, F32),
        compiler_params=_cparams("arbitrary"),
        name="moe_ffn",
    )(tile_expert, n_used, row_token, row_token, h, w_gate, w_up, w_down)


def _moe_combine_body(cur_ref, nxt_ref, x_ref, w_ref, y_hbm, o_ref, ybuf, sem, *, tc):
    i = pl.program_id(0)
    slot = i % 2
    n = TOP_K * tc

    @pl.when(i == 0)
    def _():
        _gather_rows(cur_ref, y_hbm, ybuf.at[0], sem.at[0], 0, n)

    @pl.when(i + 1 < pl.num_programs(0))
    def _():
        _gather_rows(nxt_ref, y_hbm, ybuf.at[1 - slot], sem.at[1 - slot], 0, n)

    _wait_rows(y_hbm, ybuf.at[slot], sem.at[slot], n)
    w = w_ref[...]
    y0 = _load_row_tiles(ybuf.at[slot, pl.ds(0, tc)])
    y1 = _load_row_tiles(ybuf.at[slot, pl.ds(tc, tc)])
    o_ref[...] = x_ref[...] + w[:, 0:1] * y0 + w[:, 1:2] * y1


def _moe_combine(x2d, ys, dest, w):
    t, d = x2d.shape
    tc = _tile(t, TM_COMBINE)
    nt = t // tc
    dest_tiles = dest.reshape(nt, tc, TOP_K).transpose(0, 2, 1).reshape(nt, 1, TOP_K * tc)
    return pl.pallas_call(
        functools.partial(_moe_combine_body, tc=tc),
        grid=(nt,),
        in_specs=[pl.BlockSpec((1, 1, TOP_K * tc), lambda i: (i, 0, 0), memory_space=pltpu.SMEM),
                  pl.BlockSpec((1, 1, TOP_K * tc), lambda i: (jnp.minimum(i + 1, nt - 1), 0, 0),
                               memory_space=pltpu.SMEM),
                  pl.BlockSpec((tc, d), lambda i: (i, 0)),
                  pl.BlockSpec((tc, TOP_K), lambda i: (i, 0)),
                  pl.BlockSpec(memory_space=pl.ANY)],
        out_specs=pl.BlockSpec((tc, d), lambda i: (i, 0)),
        out_shape=jax.ShapeDtypeStruct((t, d), F32),
        scratch_shapes=[pltpu.VMEM((2, TOP_K * tc) + ROW_TILE, F32), pltpu.SemaphoreType.DMA((2,))],
        compiler_params=_cparams("arbitrary"),
        name="moe_combine",
    )(dest_tiles, dest_tiles, x2d, w, ys)


def _moe(x2d, g, w_router, w_gate, w_up, w_down):
    h, idx, w = _moe_route(x2d, g, w_router)
    row_token, dest, tile_expert, n_used = _moe_plan(idx, _tile(TOP_K * x2d.shape[0], TM_MOE))
    ys = _moe_ffn(h, row_token, tile_expert, n_used, w_gate, w_up, w_down)
    return _moe_combine(x2d, ys, dest, w)


def _permute_w_in(w):
    off = np.concatenate([[0], np.cumsum(IN_SPLITS)]).tolist()
    piece = lambda i: w[:, off[i]:off[i + 1]].astype(BF16)
    order = [9, 0, 1, 3, 4, 5, 7, 8, 2, 6]
    pad = jnp.zeros((w.shape[0], D_PROJ - off[-1]), BF16)
    return jnp.concatenate([piece(i) for i in order] + [pad], axis=1)


def kernel(x, positions, g_mix, w_in, g_cq, w_uq, g_ckv, w_ukv, g_qk_q, g_qk_k, w_gla_gate_up, b_gla_gate, g_gla_out, w_pool, pool_scale, w_branch, w_out, g_ffn, w_ffn_gate, w_ffn_up, w_ffn_down, w_router, w_exp_gate, w_exp_up, w_exp_down):
    b, s, d = x.shape
    depth = g_mix.shape[0]
    for layer in range(depth):
        proj = _norm_matmul(x.reshape(b * s, d), g_mix[layer], _permute_w_in(w_in[layer]), BF16).reshape(b, s, D_PROJ)
        qT, k, vT = _mla_prep(proj, positions, g_cq[layer], w_uq[layer], g_ckv[layer], w_ukv[layer],
                              g_qk_q[layer], g_qk_k[layer])
        yaT = _attention(qT, k, vT).reshape(b, MLA_HEADS * MLA_V, s)
        yb = _gla(proj, w_gla_gate_up[layer], b_gla_gate[layer], g_gla_out[layer])
        x = _merge(x, yaT, yb, proj, w_pool[layer], pool_scale[layer], w_branch[layer], w_out[layer])
        i = layer // 2
        if layer % 2 == 0:
            x2d = _ffn(x.reshape(b * s, d), g_ffn[layer], w_ffn_gate[i], w_ffn_up[i], w_ffn_down[i])
        else:
            x2d = _moe(x.reshape(b * s, d), g_ffn[layer], w_router[i], w_exp_gate[i], w_exp_up[i], w_exp_down[i])
        x = x2d.reshape(b, s, d)
    return x
```
